```python
import jax, jax.numpy as jnp
from jax import lax
import numpy as np

D_MODEL = 2048
BATCH = 4
SEQ = 2048
DEPTH = 2

MIX_WIDTH = D_MODEL
GLA_HEADS = 4
GLA_DV = MIX_WIDTH // 2 // GLA_HEADS
GLA_DK = GLA_DV // 2
GLA_GATE_RANK = 16
GLA_GATE_TAU = 16.0
GLA_CHUNK = 64
DIL_HEADS = 8
DIL_DH = (MIX_WIDTH - GLA_HEADS * GLA_DV) // DIL_HEADS
DIL_PATTERNS = ((128, 1), (512, 4), (2048, 16))
DIL_BLOCK = 128
REL_BUCKETS = 32
REL_MAX_DIST = 2048
FFN_HIDDEN = -(-8 * D_MODEL // (3 * 256)) * 256
RMS_EPS = 1e-6
NEG_INF = -1e30

SPLIT_SIZES = (
    GLA_HEADS * GLA_DK,
    GLA_HEADS * GLA_DK,
    GLA_HEADS * GLA_DV,
    GLA_HEADS * GLA_DV,
    GLA_GATE_RANK,
    DIL_HEADS * DIL_DH,
    DIL_HEADS * DIL_DH,
    DIL_HEADS * DIL_DH,
)
N_IN = sum(SPLIT_SIZES)

kernel_name = "hybrid_gla_dilated_parallel_heads"


def rms_norm(x, g):
    xf = x.astype(jnp.float32)
    y = xf * lax.rsqrt(jnp.mean(xf * xf, axis=-1, keepdims=True) + RMS_EPS)
    return (y * g.astype(jnp.float32)).astype(x.dtype)


def t5_bucket(dist):
    max_exact = REL_BUCKETS // 2
    safe = np.maximum(dist, 1)
    large = max_exact + (np.log(safe / max_exact) / np.log(REL_MAX_DIST / max_exact)
                         * (REL_BUCKETS - max_exact)).astype(np.int64)
    large = np.minimum(large, REL_BUCKETS - 1)
    return np.where(dist < max_exact, dist, large).astype(np.int32)


def gla_mixer(q, k, v, r, log_g, onorm_g):
    Bsz, S, H, DK = q.shape
    DV = v.shape[-1]
    C = GLA_CHUNK
    N = S // C

    def chunks(t):
        return t.astype(jnp.float32).reshape(Bsz, N, C, H, t.shape[-1]).transpose(0, 3, 1, 2, 4)

    qc = chunks(q) * (DK ** -0.5)
    kc, vc, gc = chunks(k), chunks(v), chunks(log_g)
    b = jnp.cumsum(gc, axis=3)
    q_t = qc * jnp.exp(b)
    k_t = kc * jnp.exp(-b)
    causal = np.tril(np.ones((C, C), dtype=bool))
    attn = jnp.einsum('bhnid,bhnjd->bhnij', q_t, k_t)
    attn = jnp.where(causal, attn, 0.0)
    o_intra = jnp.einsum('bhnij,bhnjv->bhniv', attn, vc)

    b_last = b[:, :, :, -1, :]
    k_dec = kc * jnp.exp(b_last[:, :, :, None, :] - b)
    chunk_state = jnp.einsum('bhncd,bhncv->bhndv', k_dec, vc)

    def step(state, inp):
        q_n, decay_n, cs_n = inp
        o_n = jnp.einsum('bhcd,bhdv->bhcv', q_n, state)
        state = decay_n[..., None] * state + cs_n
        return state, o_n

    init = jnp.zeros((Bsz, H, DK, DV), jnp.float32)
    xs = (q_t.transpose(2, 0, 1, 3, 4), jnp.exp(b_last).transpose(2, 0, 1, 3),
          chunk_state.transpose(2, 0, 1, 3, 4))
    _, o_inter = lax.scan(step, init, xs)
    o = o_intra + o_inter.transpose(1, 2, 0, 3, 4)
    o = o.transpose(0, 2, 3, 1, 4).reshape(Bsz, S, H, DV)
    o = rms_norm(o, onorm_g)
    o = o.reshape(Bsz, S, H * DV) * jax.nn.silu(r.astype(jnp.float32))
    return o


def dilated_branch(q, k, v, rel_bias, window, dilation):
    Bsz, H, S, Dh = q.shape
    span = window // dilation
    Q = DIL_BLOCK
    seg = dilation * Q
    Sp = -(-S // seg) * seg
    L = Sp // dilation
    nb = L // Q

    def to_blocks(t):
        t = jnp.pad(t, ((0, 0), (0, 0), (0, Sp - S), (0, 0)))
        return t.reshape(Bsz, H, L, dilation, Dh).transpose(0, 1, 3, 2, 4).reshape(Bsz, H, dilation, nb, Q, Dh)

    qb, kb, vb = to_blocks(q), to_blocks(k), to_blocks(v)
    blk_pad = ((0, 0), (0, 0), (0, 0), (1, 0), (0, 0), (0, 0))
    kk = jnp.concatenate([jnp.pad(kb[:, :, :, :-1], blk_pad), kb], axis=4)
    vv = jnp.concatenate([jnp.pad(vb[:, :, :, :-1], blk_pad), vb], axis=4)

    i = np.arange(Q)[:, None]
    j = np.arange(2 * Q)[None, :]
    rel = Q + i - j
    band = (rel >= 0) & (rel <= span)
    valid = band[None] & ((np.arange(nb)[:, None, None] > 0) | (j >= Q)[None])
    buckets = t5_bucket(np.clip(rel, 0, None) * dilation)
    bias = jnp.take(rel_bias.astype(jnp.float32), buckets, axis=0).transpose(2, 0, 1)

    logits = jnp.einsum('bhrnqd,bhrnkd->bhrnqk', qb, kk) * (Dh ** -0.5)
    logits = logits + bias[None, :, None, None]
    logits = jnp.where(valid[None, None, None], logits, NEG_INF)
    m = jnp.max(logits, axis=-1, keepdims=True)
    p = jnp.exp(logits - m)
    s = jnp.sum(p, axis=-1, keepdims=True)
    o = jnp.einsum('bhrnqk,bhrnkd->bhrnqd', p, vv) / s
    lse = (m + jnp.log(s))[..., 0]

    o = o.reshape(Bsz, H, dilation, L, Dh).transpose(0, 1, 3, 2, 4).reshape(Bsz, H, Sp, Dh)[:, :, :S]
    lse = lse.reshape(Bsz, H, dilation, L).transpose(0, 1, 3, 2).reshape(Bsz, H, Sp)[:, :, :S]
    return o, lse


def dilated_mixer(q, k, v, rel_bias):
    Bsz, S, H, Dh = q.shape
    qf, kf, vf = (t.astype(jnp.float32).transpose(0, 2, 1, 3) for t in (q, k, v))
    outs, lses = [], []
    for window, dilation in DIL_PATTERNS:
        o, lse = dilated_branch(qf, kf, vf, rel_bias, window, dilation)
        outs.append(o)
        lses.append(lse)
    w = jax.nn.softmax(jnp.stack(lses, axis=0), axis=0)
    o = jnp.sum(w[..., None] * jnp.stack(outs, axis=0), axis=0)
    return o.transpose(0, 2, 1, 3).reshape(Bsz, S, H * Dh)


def setup_inputs(seed: int = 0) -> dict:
    key = jax.random.key(seed)
    ks = jax.random.split(key, 16)
    f32 = jnp.float32
    nrm = lambda k, shape, scale: jax.random.normal(k, shape, f32) * scale
    return {
        "x": nrm(ks[0], (BATCH, SEQ, D_MODEL), 1.0),
        "norm1_g": 1.0 + nrm(ks[1], (DEPTH, D_MODEL), 0.02),
        "w_in": nrm(ks[2], (DEPTH, D_MODEL, N_IN), D_MODEL ** -0.5),
        "gla_gate_w2": nrm(ks[3], (DEPTH, GLA_GATE_RANK, GLA_HEADS * GLA_DK), GLA_GATE_RANK ** -0.5),
        "gla_gate_b": nrm(ks[4], (DEPTH, GLA_HEADS * GLA_DK), 0.1),
        "gla_onorm_g": 1.0 + nrm(ks[5], (DEPTH, GLA_DV), 0.02),
        "q_norm_g": 1.0 + nrm(ks[6], (DEPTH, DIL_DH), 0.02),
        "k_norm_g": 1.0 + nrm(ks[7], (DEPTH, DIL_DH), 0.02),
        "rel_bias": nrm(ks[8], (REL_BUCKETS, DIL_HEADS), 0.5),
        "w_out": nrm(ks[9], (DEPTH, MIX_WIDTH, D_MODEL), MIX_WIDTH ** -0.5),
        "norm2_g": 1.0 + nrm(ks[10], (DEPTH, D_MODEL), 0.02),
        "w_gate": nrm(ks[11], (DEPTH, D_MODEL, FFN_HIDDEN), D_MODEL ** -0.5),
        "w_up": nrm(ks[12], (DEPTH, D_MODEL, FFN_HIDDEN), D_MODEL ** -0.5),
        "w_down": nrm(ks[13], (DEPTH, FFN_HIDDEN, D_MODEL), FFN_HIDDEN ** -0.5),
    }


def reference(x, norm1_g, w_in, gla_gate_w2, gla_gate_b, gla_onorm_g, q_norm_g, k_norm_g,
              rel_bias, w_out, norm2_g, w_gate, w_up, w_down):
    Bsz, S, _ = x.shape
    split_points = list(np.cumsum(SPLIT_SIZES)[:-1])
    for l in range(DEPTH):
        h = rms_norm(x, norm1_g[l])
        proj = h @ w_in[l]
        qa, ka, va, ra, ga, qb, kb, vb = jnp.split(proj, split_points, axis=-1)
        gate_pre = (ga @ gla_gate_w2[l] + gla_gate_b[l]).astype(jnp.float32)
        log_g = jax.nn.log_sigmoid(gate_pre) / GLA_GATE_TAU
        out_a = gla_mixer(qa.reshape(Bsz, S, GLA_HEADS, GLA_DK),
                          ka.reshape(Bsz, S, GLA_HEADS, GLA_DK),
                          va.reshape(Bsz, S, GLA_HEADS, GLA_DV),
                          ra,
                          log_g.reshape(Bsz, S, GLA_HEADS, GLA_DK),
                          gla_onorm_g[l])
        qb = rms_norm(qb.reshape(Bsz, S, DIL_HEADS, DIL_DH), q_norm_g[l])
        kb = rms_norm(kb.reshape(Bsz, S, DIL_HEADS, DIL_DH), k_norm_g[l])
        out_b = dilated_mixer(qb, kb, vb.reshape(Bsz, S, DIL_HEADS, DIL_DH), rel_bias)
        mix = jnp.concatenate([out_a.astype(x.dtype), out_b.astype(x.dtype)], axis=-1)
        x = x + mix @ w_out[l]
        h2 = rms_norm(x, norm2_g[l])
        x = x + (jax.nn.silu(h2 @ w_gate[l]) * (h2 @ w_up[l])) @ w_down[l]
    return x
```

```python
import functools

import numpy as np
import jax
import jax.numpy as jnp
from jax import lax
from jax.experimental import pallas as pl
from jax.experimental.pallas import tpu as pltpu

D_MODEL = 2048
BATCH = 4
SEQ = 2048
DEPTH = 2
M_TOK = BATCH * SEQ

GLA_HEADS = 4
GLA_DV = 256
GLA_DK = 128
GLA_GATE_RANK = 16
GLA_GATE_TAU = 16.0
GLA_CHUNK = 64
DIL_HEADS = 8
DIL_DH = 128
DIL_PATTERNS = ((128, 1), (512, 4), (2048, 16))
DIL_BLOCK = 128
REL_BUCKETS = 32
REL_MAX_DIST = 2048
FFN_HIDDEN = 5632
RMS_EPS = 1e-6
NEG_INF = -1e30

QA_W = GLA_HEADS * GLA_DK
VA_W = GLA_HEADS * GLA_DV
DIL_W = DIL_HEADS * DIL_DH
PA_QA, PA_KA, PA_VA, PA_RA, PA_VB = 0, QA_W, 2 * QA_W, 2 * QA_W + VA_W, 2 * QA_W + 2 * VA_W
PA_W = PA_VB + DIL_W
QK_W = 2 * DIL_W
LANE = 128
GATE_PAD = LANE

BF16 = jnp.bfloat16
F32 = jnp.float32

VMEM_LIMIT = 56 * 1024 * 1024

TM_NORM = 512
TM_PROJ, TN_PROJ = 1024, 1024
TM_GATE = 1024
TM_OUT = 256
TM_FFN, TF_FFN = 512, 512
DIL_HEADS_PER_STEP = {1: 2, 4: 4, 16: 8}


def _cparams(sem):
    return pltpu.CompilerParams(dimension_semantics=sem, vmem_limit_bytes=VMEM_LIMIT)


def _rmsnorm_kernel(x_ref, g_ref, o_ref):
    x = x_ref[...]
    ms = jnp.mean(x * x, axis=-1, keepdims=True)
    o_ref[...] = (x * lax.rsqrt(ms + RMS_EPS) * g_ref[...]).astype(o_ref.dtype)


def _rmsnorm(x, g):
    m, d = x.shape
    return pl.pallas_call(
        _rmsnorm_kernel,
        grid=(m // TM_NORM,),
        in_specs=[pl.BlockSpec((TM_NORM, d), lambda i: (i, 0)),
                  pl.BlockSpec((1, d), lambda i: (0, 0))],
        out_specs=pl.BlockSpec((TM_NORM, d), lambda i: (i, 0)),
        out_shape=jax.ShapeDtypeStruct((m, d), BF16),
        compiler_params=_cparams(("parallel",)),
        name="rmsnorm",
    )(x, g.reshape(1, d))


def _proj_kernel(h_ref, w_ref, o_ref):
    o_ref[...] = jnp.dot(h_ref[...], w_ref[...], preferred_element_type=F32).astype(o_ref.dtype)


def _proj(h, w):
    m, k = h.shape
    n = w.shape[1]
    return pl.pallas_call(
        _proj_kernel,
        grid=(m // TM_PROJ, n // TN_PROJ),
        in_specs=[pl.BlockSpec((TM_PROJ, k), lambda i, j: (i, 0)),
                  pl.BlockSpec((k, TN_PROJ), lambda i, j: (0, j))],
        out_specs=pl.BlockSpec((TM_PROJ, TN_PROJ), lambda i, j: (i, j)),
        out_shape=jax.ShapeDtypeStruct((m, n), BF16),
        compiler_params=_cparams(("parallel", "arbitrary")),
        name="proj_plain",
    )(h, w)


def _proj_qknorm_kernel(h_ref, w_ref, g_ref, o_ref):
    acc = jnp.dot(h_ref[...], w_ref[...], preferred_element_type=F32)
    for hh in range(acc.shape[1] // DIL_DH):
        cs = slice(hh * DIL_DH, (hh + 1) * DIL_DH)
        a = acc[:, cs]
        ms = jnp.mean(a * a, axis=-1, keepdims=True)
        o_ref[:, cs] = (a * lax.rsqrt(ms + RMS_EPS) * g_ref[:, cs]).astype(o_ref.dtype)


def _proj_qknorm(h, w, g):
    m, k = h.shape
    n = w.shape[1]
    return pl.pallas_call(
        _proj_qknorm_kernel,
        grid=(m // TM_PROJ, n // TN_PROJ),
        in_specs=[pl.BlockSpec((TM_PROJ, k), lambda i, j: (i, 0)),
                  pl.BlockSpec((k, TN_PROJ), lambda i, j: (0, j)),
                  pl.BlockSpec((1, TN_PROJ), lambda i, j: (0, j))],
        out_specs=pl.BlockSpec((TM_PROJ, TN_PROJ), lambda i, j: (i, j)),
        out_shape=jax.ShapeDtypeStruct((m, n), BF16),
        compiler_params=_cparams(("parallel", "arbitrary")),
        name="proj_qknorm",
    )(h, w, g)


def _gate_kernel(h_ref, w1_ref, w2_ref, b_ref, o_ref):
    ga = jnp.dot(h_ref[...], w1_ref[...], preferred_element_type=F32)
    pre = jnp.dot(ga.astype(BF16), w2_ref[...], preferred_element_type=F32) + b_ref[...]
    o_ref[...] = jax.nn.log_sigmoid(pre) * (1.0 / GLA_GATE_TAU)


def _gate(h, w1, w2, b):
    m, k = h.shape
    n = w2.shape[1]
    return pl.pallas_call(
        _gate_kernel,
        grid=(m // TM_GATE,),
        in_specs=[pl.BlockSpec((TM_GATE, k), lambda i: (i, 0)),
                  pl.BlockSpec((k, GATE_PAD), lambda i: (0, 0)),
                  pl.BlockSpec((GATE_PAD, n), lambda i: (0, 0)),
                  pl.BlockSpec((1, n), lambda i: (0, 0))],
        out_specs=pl.BlockSpec((TM_GATE, n), lambda i: (i, 0)),
        out_shape=jax.ShapeDtypeStruct((m, n), F32),
        compiler_params=_cparams(("parallel",)),
        name="gla_gate",
    )(h, w1, w2, b)


def _gla_kernel(q_ref, k_ref, v_ref, r_ref, lg_ref, gn_ref, o_ref,
                qt_s, kt_s, kd_s, dec_s, st_s, o_s):
    s_len, c, n_chunks = SEQ, GLA_CHUNK, SEQ // GLA_CHUNK
    b = lg_ref[...]
    row = lax.broadcasted_iota(jnp.int32, b.shape, 0) & (c - 1)
    shift = 1
    while shift < c:
        b = b + jnp.where(row >= shift, pltpu.roll(b, shift, axis=0), 0.0)
        shift *= 2
    b3 = b.reshape(n_chunks, c, GLA_DK)
    b_last = jnp.broadcast_to(b3[:, c - 1:c, :], b3.shape).reshape(s_len, GLA_DK)
    q = q_ref[...].astype(F32) * (GLA_DK ** -0.5)
    k = k_ref[...].astype(F32)
    qt_s[...] = (q * jnp.exp(b)).astype(BF16)
    kt_s[...] = (k * jnp.exp(-b)).astype(BF16)
    kd_s[...] = (k * jnp.exp(b_last - b)).astype(BF16)
    dec_s[...] = jnp.exp(b_last)
    st_s[...] = jnp.zeros_like(st_s)

    ri = lax.broadcasted_iota(jnp.int32, (c, c), 0)
    ci = lax.broadcasted_iota(jnp.int32, (c, c), 1)
    causal = ri >= ci

    def chunk(n, carry):
        r0 = pl.multiple_of(n * c, c)
        rows = pl.ds(r0, c)
        qt = qt_s[rows, :]
        kt = kt_s[rows, :]
        kd = kd_s[rows, :]
        v = v_ref[rows, :]
        st = st_s[...]
        attn = lax.dot_general(qt, kt, (((1,), (1,)), ((), ())), preferred_element_type=F32)
        attn = jnp.where(causal, attn, 0.0).astype(BF16)
        o = jnp.dot(attn, v, preferred_element_type=F32)
        o = o + lax.dot_general(qt, st.astype(BF16), (((1,), (1,)), ((), ())),
                                preferred_element_type=F32)
        cs_t = lax.dot_general(v, kd, (((0,), (0,)), ((), ())), preferred_element_type=F32)
        dec = dec_s[pl.ds(r0, 8), :][0:1, :]
        st_s[...] = st * dec + cs_t
        o_s[rows, :] = o
        return carry

    lax.fori_loop(0, n_chunks, chunk, 0, unroll=2)

    o = o_s[...]
    ms = jnp.mean(o * o, axis=-1, keepdims=True)
    y = o * lax.rsqrt(ms + RMS_EPS) * gn_ref[...]
    r = r_ref[...].astype(F32)
    o_ref[...] = (y * (r * jax.nn.sigmoid(r))).astype(o_ref.dtype)


def _gla(pa, log_g, onorm_g):
    s = SEQ
    qb_, vb_ = QA_W // GLA_DK, VA_W // GLA_DV
    return pl.pallas_call(
        _gla_kernel,
        grid=(BATCH, GLA_HEADS),
        in_specs=[
            pl.BlockSpec((s, GLA_DK), lambda b, h: (b, PA_QA // GLA_DK + h)),
            pl.BlockSpec((s, GLA_DK), lambda b, h: (b, PA_KA // GLA_DK + h)),
            pl.BlockSpec((s, GLA_DV), lambda b, h: (b, PA_VA // GLA_DV + h)),
            pl.BlockSpec((s, GLA_DV), lambda b, h: (b, PA_RA // GLA_DV + h)),
            pl.BlockSpec((s, GLA_DK), lambda b, h: (b, h)),
            pl.BlockSpec((1, GLA_DV), lambda b, h: (0, 0)),
        ],
        out_specs=pl.BlockSpec((s, GLA_DV), lambda b, h: (b, h)),
        out_shape=jax.ShapeDtypeStruct((M_TOK, VA_W), BF16),
        scratch_shapes=[
            pltpu.VMEM((s, GLA_DK), BF16),
            pltpu.VMEM((s, GLA_DK), BF16),
            pltpu.VMEM((s, GLA_DK), BF16),
            pltpu.VMEM((s, GLA_DK), F32),
            pltpu.VMEM((GLA_DV, GLA_DK), F32),
            pltpu.VMEM((s, GLA_DV), F32),
        ],
        compiler_params=_cparams(("parallel", "parallel")),
        name="gla_mixer",
    )(pa, pa, pa, pa, log_g, onorm_g.reshape(1, GLA_DV))


def _t5_bucket(dist):
    max_exact = REL_BUCKETS // 2
    safe = np.maximum(dist, 1)
    large = max_exact + (np.log(safe / max_exact) / np.log(REL_MAX_DIST / max_exact)
                         * (REL_BUCKETS - max_exact)).astype(np.int64)
    large = np.minimum(large, REL_BUCKETS - 1)
    return np.where(dist < max_exact, dist, large).astype(np.int32)


def _block_geometry():
    q = DIL_BLOCK
    i = np.arange(q)[:, None]
    j = np.arange(2 * q)[None, :]
    rel = q + i - j
    buckets, bands = [], []
    for window, dilation in DIL_PATTERNS:
        span = window // dilation
        bands.append((rel >= 0) & (rel <= span))
        buckets.append(_t5_bucket(np.clip(rel, 0, None) * dilation))
    assert all(np.array_equal(bands[0], bnd) for bnd in bands)
    return np.stack(buckets).astype(np.int32), bands[0].astype(np.float32)


def _bias_kernel(rb_ref, bk_ref, o_ref):
    bk = bk_ref[...]
    for h in range(DIL_HEADS):
        acc = jnp.zeros(bk.shape, F32)
        for bucket in range(REL_BUCKETS):
            acc = jnp.where(bk == bucket, rb_ref[bucket, h], acc)
        o_ref[h] = acc


def _bias_tables(rel_bias, buckets):
    nbr, q, q2 = buckets.shape
    return pl.pallas_call(
        _bias_kernel,
        grid=(nbr,),
        in_specs=[pl.BlockSpec(memory_space=pltpu.SMEM),
                  pl.BlockSpec((None, q, q2), lambda i: (i, 0, 0))],
        out_specs=pl.BlockSpec((None, DIL_HEADS, q, q2), lambda i: (i, 0, 0, 0)),
        out_shape=jax.ShapeDtypeStruct((nbr, DIL_HEADS, q, q2), F32),
        compiler_params=_cparams(("parallel",)),
        name="t5_bias_tables",
    )(rel_bias, buckets)


def _dil_kernel(q_ref, k_ref, v_ref, bias_ref, band_ref, o_ref, lse_ref, *, seq_len, heads):
    qb = DIL_BLOCK
    n_blocks = seq_len // qb
    g = pl.program_id(2)

    @pl.when(g == 0)
    def _():
        lse_ref[...] = jnp.zeros_like(lse_ref)

    lane = lax.broadcasted_iota(jnp.int32, (qb, LANE), 1)
    band = band_ref[...] > 0.5

    def attend(qi, kw, vw, bias, valid):
        s = lax.dot_general(qi, kw, (((1,), (1,)), ((), ())), preferred_element_type=F32)
        s = jnp.where(valid, s + bias, NEG_INF)
        m = jnp.max(s, axis=-1, keepdims=True)
        p = jnp.exp(s - m)
        l = jnp.sum(p, axis=-1, keepdims=True)
        o = jnp.dot(p.astype(BF16), vw, preferred_element_type=F32) / l
        return o, m + jnp.log(l)

    for hh in range(heads):
        cs = slice(hh * DIL_DH, (hh + 1) * DIL_DH)
        head = g * heads + hh
        bias = bias_ref[hh]

        def put(rows, o, lse):
            o_ref[rows, cs] = o
            lse_ref[rows, :] = jnp.where(lane == head, lse, lse_ref[rows, :])

        r0 = slice(0, qb)
        o, lse = attend(q_ref[r0, cs], k_ref[r0, cs], v_ref[r0, cs], bias[:, qb:], band[:, qb:])
        put(r0, o, lse)

        def block(i, carry):
            q0 = pl.multiple_of(i * qb, qb)
            k0 = pl.multiple_of((i - 1) * qb, qb)
            rows, kwin = pl.ds(q0, qb), pl.ds(k0, 2 * qb)
            o, lse = attend(q_ref[rows, cs], k_ref[kwin, cs], v_ref[kwin, cs], bias, band)
            put(rows, o, lse)
            return carry

        if n_blocks > 1:
            lax.fori_loop(1, n_blocks, block, 0)


def _dilated_branch(qk, pa, bias_tabs, band, branch):
    _, d = DIL_PATTERNS[branch]
    seq_len = SEQ // d
    heads = DIL_HEADS_PER_STEP[d]
    cw = heads * DIL_DH
    groups = DIL_HEADS // heads
    qk_v = qk.reshape(BATCH, seq_len, d * QK_W)
    pa_v = pa.reshape(BATCH, seq_len, d * PA_W)
    o, lse = pl.pallas_call(
        functools.partial(_dil_kernel, seq_len=seq_len, heads=heads),
        grid=(BATCH, d, groups),
        in_specs=[
            pl.BlockSpec((None, seq_len, cw), lambda b, r, g: (b, 0, r * (QK_W // cw) + g)),
            pl.BlockSpec((None, seq_len, cw), lambda b, r, g: (b, 0, r * (QK_W // cw) + DIL_W // cw + g)),
            pl.BlockSpec((None, seq_len, cw), lambda b, r, g: (b, 0, r * (PA_W // cw) + PA_VB // cw + g)),
            pl.BlockSpec((None, heads, DIL_BLOCK, 2 * DIL_BLOCK), lambda b, r, g: (branch, g, 0, 0)),
            pl.BlockSpec((DIL_BLOCK, 2 * DIL_BLOCK), lambda b, r, g: (0, 0)),
        ],
        out_specs=[
            pl.BlockSpec((None, seq_len, cw), lambda b, r, g: (b, 0, r * groups + g)),
            pl.BlockSpec((None, seq_len, LANE), lambda b, r, g: (b, 0, r)),
        ],
        out_shape=[jax.ShapeDtypeStruct((BATCH, seq_len, d * DIL_W), F32),
                   jax.ShapeDtypeStruct((BATCH, seq_len, d * LANE), F32)],
        compiler_params=_cparams(("parallel", "parallel", "arbitrary")),
        name=f"dilated_branch_d{d}",
    )(qk_v, qk_v, pa_v, bias_tabs, band)
    return o.reshape(M_TOK, DIL_W), lse.reshape(M_TOK, LANE)


def _outproj_kernel(oa_ref, o1_ref, o2_ref, o3_ref, l1_ref, l2_ref, l3_ref, x_ref, w_ref, g_ref,
                    x1_ref, h2_ref):
    l1, l2, l3 = l1_ref[...], l2_ref[...], l3_ref[...]
    mx = jnp.maximum(jnp.maximum(l1, l2), l3)
    e1, e2, e3 = jnp.exp(l1 - mx), jnp.exp(l2 - mx), jnp.exp(l3 - mx)
    den = e1 + e2 + e3
    w1, w2, w3 = e1 / den, e2 / den, e3 / den
    acc = jnp.dot(oa_ref[...], w_ref[0:VA_W, :], preferred_element_type=F32)
    parts = []
    for h in range(DIL_HEADS):
        cs = slice(h * DIL_DH, (h + 1) * DIL_DH)
        mix = (w1[:, h:h + 1] * o1_ref[:, cs] + w2[:, h:h + 1] * o2_ref[:, cs]
               + w3[:, h:h + 1] * o3_ref[:, cs])
        parts.append(mix.astype(BF16))
    mix_b = jnp.concatenate(parts, axis=-1)
    acc = acc + jnp.dot(mix_b, w_ref[VA_W:, :], preferred_element_type=F32)
    x1 = x_ref[...] + acc
    x1_ref[...] = x1
    ms = jnp.mean(x1 * x1, axis=-1, keepdims=True)
    h2_ref[...] = (x1 * lax.rsqrt(ms + RMS_EPS) * g_ref[...]).astype(h2_ref.dtype)


def _outproj(out_a, o_parts, lse_parts, x, w_out, norm2_g):
    m, d = x.shape
    tm = TM_OUT
    row = lambda w: pl.BlockSpec((tm, w), lambda i: (i, 0))
    return pl.pallas_call(
        _outproj_kernel,
        grid=(m // tm,),
        in_specs=[row(VA_W), row(DIL_W), row(DIL_W), row(DIL_W), row(LANE), row(LANE), row(LANE),
                  row(d),
                  pl.BlockSpec((d, d), lambda i: (0, 0)),
                  pl.BlockSpec((1, d), lambda i: (0, 0))],
        out_specs=[row(d), row(d)],
        out_shape=[jax.ShapeDtypeStruct((m, d), F32), jax.ShapeDtypeStruct((m, d), BF16)],
        compiler_params=_cparams(("parallel",)),
        name="merge_outproj_norm2",
    )(out_a, *o_parts, *lse_parts, x, w_out, norm2_g.reshape(1, d))


def _ffn_kernel(h_ref, x_ref, wg_ref, wu_ref, wd_ref, o_ref):
    j = pl.program_id(1)

    @pl.when(j == 0)
    def _():
        o_ref[...] = x_ref[...]

    h = h_ref[...]
    gate = jnp.dot(h, wg_ref[...], preferred_element_type=F32)
    up = jnp.dot(h, wu_ref[...], preferred_element_type=F32)
    act = (gate * jax.nn.sigmoid(gate) * up).astype(BF16)
    o_ref[...] += jnp.dot(act, wd_ref[...], preferred_element_type=F32)


def _ffn(h2, x1, w_gate, w_up, w_down):
    m, d = x1.shape
    f = w_gate.shape[1]
    tm, tf = TM_FFN, TF_FFN
    return pl.pallas_call(
        _ffn_kernel,
        grid=(m // tm, f // tf),
        in_specs=[pl.BlockSpec((tm, d), lambda i, j: (i, 0)),
                  pl.BlockSpec((tm, d), lambda i, j: (i, 0)),
                  pl.BlockSpec((d, tf), lambda i, j: (0, j)),
                  pl.BlockSpec((d, tf), lambda i, j: (0, j)),
                  pl.BlockSpec((tf, d), lambda i, j: (j, 0))],
        out_specs=pl.BlockSpec((tm, d), lambda i, j: (i, 0)),
        out_shape=jax.ShapeDtypeStruct((m, d), F32),
        compiler_params=_cparams(("parallel", "arbitrary")),
        name="swiglu_ffn",
    )(h2, x1, w_gate, w_up, w_down)


def _split_w_in(w_in_l):
    edges = np.cumsum((0, QA_W, QA_W, VA_W, VA_W, GLA_GATE_RANK, DIL_W, DIL_W, DIL_W))
    qa_ra = w_in_l[:, edges[0]:edges[4]]
    ga = w_in_l[:, edges[4]:edges[5]]
    qk = w_in_l[:, edges[5]:edges[7]]
    vb = w_in_l[:, edges[7]:edges[8]]
    w_plain = jnp.concatenate([qa_ra, vb], axis=1).astype(BF16)
    w_gate1 = jnp.pad(ga, ((0, 0), (0, GATE_PAD - GLA_GATE_RANK))).astype(BF16)
    return w_plain, w_gate1, qk.astype(BF16)


def kernel(x, norm1_g, w_in, gla_gate_w2, gla_gate_b, gla_onorm_g, q_norm_g, k_norm_g, rel_bias,
           w_out, norm2_g, w_gate, w_up, w_down):
    bsz, s, d = x.shape
    assert (bsz, s, d) == (BATCH, SEQ, D_MODEL)
    buckets, band = _block_geometry()
    bias_tabs = _bias_tables(rel_bias.astype(F32), jnp.asarray(buckets))
    band = jnp.asarray(band)

    xf = x.reshape(M_TOK, D_MODEL)
    for l in range(DEPTH):
        w_plain, w_gate1, w_qk = _split_w_in(w_in[l])
        w_gate2 = jnp.pad(gla_gate_w2[l], ((0, GATE_PAD - GLA_GATE_RANK), (0, 0))).astype(BF16)
        qk_gain = jnp.concatenate([jnp.tile(q_norm_g[l], DIL_HEADS) * (DIL_DH ** -0.5),
                                   jnp.tile(k_norm_g[l], DIL_HEADS)]).reshape(1, QK_W).astype(F32)

        h = _rmsnorm(xf, norm1_g[l])
        pa = _proj(h, w_plain)
        qk = _proj_qknorm(h, w_qk, qk_gain)
        log_g = _gate(h, w_gate1, w_gate2, gla_gate_b[l].reshape(1, QA_W).astype(F32))

        out_a = _gla(pa, log_g, gla_onorm_g[l])
        o_parts, lse_parts = [], []
        for branch in range(len(DIL_PATTERNS)):
            o_i, lse_i = _dilated_branch(qk, pa, bias_tabs, band, branch)
            o_parts.append(o_i)
            lse_parts.append(lse_i)

        x1, h2 = _outproj(out_a, o_parts, lse_parts, xf, w_out[l].astype(BF16), norm2_g[l])
        xf = _ffn(h2, x1, w_gate[l].astype(BF16), w_up[l].astype(BF16), w_down[l].astype(BF16))
    return xf.reshape(bsz, s, d)
```

```python
import functools

import numpy as np
import jax
import jax.numpy as jnp
from jax import lax
from jax.experimental import pallas as pl
from jax.experimental.pallas import tpu as pltpu

D_MODEL = 2048
BATCH = 4
SEQ = 2048
DEPTH = 2
M_TOK = BATCH * SEQ

GLA_HEADS = 4
GLA_DV = 256
GLA_DK = 128
GLA_GATE_RANK = 16
GLA_GATE_TAU = 16.0
GLA_CHUNK = 64
DIL_HEADS = 8
DIL_DH = 128
DIL_PATTERNS = ((128, 1), (512, 4), (2048, 16))
DIL_BLOCK = 128
REL_BUCKETS = 32
REL_MAX_DIST = 2048
FFN_HIDDEN = 5632
RMS_EPS = 1e-6

QA_W = GLA_HEADS * GLA_DK
VA_W = GLA_HEADS * GLA_DV
DIL_W = DIL_HEADS * DIL_DH
PA_QA, PA_KA, PA_VA, PA_RA = 0, QA_W, 2 * QA_W, 2 * QA_W + VA_W
PA_W = PA_RA + VA_W
GA_OFF = PA_W
PB_OFF = GA_OFF + GLA_GATE_RANK
PB_W = 3 * DIL_W
LANE = 128
F32_ROWS = 8
GATE_PAD = LANE

N_CLASS = 16
CLASS_LEN = SEQ // N_CLASS

BF16 = jnp.bfloat16
F32 = jnp.float32

VMEM_LIMIT = 56 * 1024 * 1024

TM_NORM = 512
TM_PROJ, TN_PROJ = 1024, 1024
TM_GATE = 1024
TM_OUT = 256
TM_FFN, TF_FFN = 512, 512
GLA_UNROLL = 4
ATT_GROUP = 4


def _cparams(sem):
    return pltpu.CompilerParams(dimension_semantics=sem, vmem_limit_bytes=VMEM_LIMIT)


def _rmsnorm_kernel(x_ref, g_ref, o_ref):
    x = x_ref[...]
    ms = jnp.mean(x * x, axis=-1, keepdims=True)
    o_ref[...] = (x * lax.rsqrt(ms + RMS_EPS) * g_ref[...]).astype(o_ref.dtype)


def _rmsnorm(x, g):
    m, d = x.shape
    return pl.pallas_call(
        _rmsnorm_kernel,
        grid=(m // TM_NORM,),
        in_specs=[pl.BlockSpec((TM_NORM, d), lambda i: (i, 0)),
                  pl.BlockSpec((1, d), lambda i: (0, 0))],
        out_specs=pl.BlockSpec((TM_NORM, d), lambda i: (i, 0)),
        out_shape=jax.ShapeDtypeStruct((m, d), BF16),
        compiler_params=_cparams(("parallel",)),
        name="rmsnorm",
    )(x, g.reshape(1, d))


def _proj_plain_kernel(h_ref, w_ref, o_ref, wb_s):
    @pl.when(pl.program_id(1) == 0)
    def _():
        wb_s[...] = w_ref[...].astype(BF16)

    o_ref[...] = jnp.dot(h_ref[...], wb_s[...], preferred_element_type=F32).astype(o_ref.dtype)


def _proj_plain(h, w_in, layer):
    m, k = h.shape
    return pl.pallas_call(
        _proj_plain_kernel,
        grid=(PA_W // TN_PROJ, m // TM_PROJ),
        in_specs=[pl.BlockSpec((TM_PROJ, k), lambda j, i: (i, 0)),
                  pl.BlockSpec((None, k, TN_PROJ), lambda j, i: (layer, 0, j))],
        out_specs=pl.BlockSpec((TM_PROJ, TN_PROJ), lambda j, i: (i, j)),
        out_shape=jax.ShapeDtypeStruct((m, PA_W), BF16),
        scratch_shapes=[pltpu.VMEM((k, TN_PROJ), BF16)],
        compiler_params=_cparams(("parallel", "arbitrary")),
        name="proj_plain",
    )(h, w_in)


def _proj_qkv_kernel(h_ref, w_ref, g_ref, o_ref):
    acc = jnp.dot(h_ref[...], w_ref[...], preferred_element_type=F32)
    j = pl.program_id(1)

    @pl.when(j < 2)
    def _():
        for hh in range(acc.shape[1] // DIL_DH):
            cs = slice(hh * DIL_DH, (hh + 1) * DIL_DH)
            a = acc[:, cs]
            ms = jnp.mean(a * a, axis=-1, keepdims=True)
            o_ref[:, cs] = (a * lax.rsqrt(ms + RMS_EPS) * g_ref[:, cs]).astype(o_ref.dtype)

    @pl.when(j == 2)
    def _():
        o_ref[...] = acc.astype(o_ref.dtype)


def _proj_qkv(h, w, g):
    m, k = h.shape
    assert TN_PROJ == DIL_W
    return pl.pallas_call(
        _proj_qkv_kernel,
        grid=(m // TM_PROJ, PB_W // TN_PROJ),
        in_specs=[pl.BlockSpec((TM_PROJ, k), lambda i, j: (i, 0)),
                  pl.BlockSpec((k, TN_PROJ), lambda i, j: (0, j)),
                  pl.BlockSpec((1, TN_PROJ), lambda i, j: (0, jnp.minimum(j, 1)))],
        out_specs=pl.BlockSpec((TM_PROJ, TN_PROJ), lambda i, j: (i, j)),
        out_shape=jax.ShapeDtypeStruct((m, PB_W), BF16),
        compiler_params=_cparams(("parallel", "arbitrary")),
        name="proj_qkv",
    )(h, w, g)


def _gate_kernel(h_ref, w1_ref, w2_ref, b_ref, o_ref):
    ga = jnp.dot(h_ref[...], w1_ref[...], preferred_element_type=F32)
    pre = jnp.dot(ga.astype(BF16), w2_ref[...], preferred_element_type=F32) + b_ref[...]
    o_ref[...] = jax.nn.log_sigmoid(pre) * (1.0 / GLA_GATE_TAU)


def _gate(h, w1, w2, b):
    m, k = h.shape
    n = w2.shape[1]
    return pl.pallas_call(
        _gate_kernel,
        grid=(m // TM_GATE,),
        in_specs=[pl.BlockSpec((TM_GATE, k), lambda i: (i, 0)),
                  pl.BlockSpec((k, GATE_PAD), lambda i: (0, 0)),
                  pl.BlockSpec((GATE_PAD, n), lambda i: (0, 0)),
                  pl.BlockSpec((1, n), lambda i: (0, 0))],
        out_specs=pl.BlockSpec((TM_GATE, n), lambda i: (i, 0)),
        out_shape=jax.ShapeDtypeStruct((m, n), F32),
        compiler_params=_cparams(("parallel",)),
        name="gla_gate",
    )(h, w1, w2, b)


def _gla_kernel(q_ref, k_ref, v_ref, r_ref, lg_ref, gn_ref, o_ref,
                qt_s, kt_s, kd_s, dec_s, st_s, o_s):
    s_len, c, n_chunks = SEQ, GLA_CHUNK, SEQ // GLA_CHUNK
    b = lg_ref[...]
    row = lax.broadcasted_iota(jnp.int32, b.shape, 0) & (c - 1)
    shift = 1
    while shift < c:
        b = b + jnp.where(row >= shift, pltpu.roll(b, shift, axis=0), 0.0)
        shift *= 2
    b3 = b.reshape(n_chunks, c, GLA_DK)
    b_last = jnp.broadcast_to(b3[:, c - 1:c, :], b3.shape).reshape(s_len, GLA_DK)
    q = q_ref[...].astype(F32) * (GLA_DK ** -0.5)
    k = k_ref[...].astype(F32)
    qt_s[...] = (q * jnp.exp(b)).astype(BF16)
    kt_s[...] = (k * jnp.exp(-b)).astype(BF16)
    kd_s[...] = (k * jnp.exp(b_last - b)).astype(BF16)
    dec_s[...] = jnp.exp(b_last)
    st_s[...] = jnp.zeros_like(st_s)

    ri = lax.broadcasted_iota(jnp.int32, (c, c), 0)
    ci = lax.broadcasted_iota(jnp.int32, (c, c), 1)
    causal = ri >= ci

    def chunk(n, carry):
        r0 = pl.multiple_of(n * c, c)
        rows = pl.ds(r0, c)
        qt = qt_s[rows, :]
        kt = kt_s[rows, :]
        kd = kd_s[rows, :]
        v = v_ref[rows, :]
        st = st_s[...]
        attn = lax.dot_general(qt, kt, (((1,), (1,)), ((), ())), preferred_element_type=F32)
        attn = jnp.where(causal, attn, 0.0).astype(BF16)
        o = jnp.dot(attn, v, preferred_element_type=F32)
        o = o + lax.dot_general(qt, st.astype(BF16), (((1,), (1,)), ((), ())),
                                preferred_element_type=F32)
        cs_t = lax.dot_general(v, kd, (((0,), (0,)), ((), ())), preferred_element_type=F32)
        dec = dec_s[pl.ds(r0, F32_ROWS), :][0:1, :]
        st_s[...] = st * dec + cs_t
        o_s[rows, :] = o
        return carry

    lax.fori_loop(0, n_chunks, chunk, 0, unroll=GLA_UNROLL)

    o = o_s[...]
    ms = jnp.mean(o * o, axis=-1, keepdims=True)
    y = o * lax.rsqrt(ms + RMS_EPS) * gn_ref[...]
    r = r_ref[...].astype(F32)
    o_ref[...] = (y * (r * jax.nn.sigmoid(r))).astype(o_ref.dtype)


def _gla(pa, log_g, onorm_g):
    s = SEQ
    return pl.pallas_call(
        _gla_kernel,
        grid=(BATCH, GLA_HEADS),
        in_specs=[
            pl.BlockSpec((s, GLA_DK), lambda b, h: (b, PA_QA // GLA_DK + h)),
            pl.BlockSpec((s, GLA_DK), lambda b, h: (b, PA_KA // GLA_DK + h)),
            pl.BlockSpec((s, GLA_DV), lambda b, h: (b, PA_VA // GLA_DV + h)),
            pl.BlockSpec((s, GLA_DV), lambda b, h: (b, PA_RA // GLA_DV + h)),
            pl.BlockSpec((s, GLA_DK), lambda b, h: (b, h)),
            pl.BlockSpec((1, GLA_DV), lambda b, h: (0, 0)),
        ],
        out_specs=pl.BlockSpec((s, GLA_DV), lambda b, h: (b, h)),
        out_shape=jax.ShapeDtypeStruct((M_TOK, VA_W), BF16),
        scratch_shapes=[
            pltpu.VMEM((s, GLA_DK), BF16),
            pltpu.VMEM((s, GLA_DK), BF16),
            pltpu.VMEM((s, GLA_DK), BF16),
            pltpu.VMEM((s, GLA_DK), F32),
            pltpu.VMEM((GLA_DV, GLA_DK), F32),
            pltpu.VMEM((s, GLA_DV), F32),
        ],
        compiler_params=_cparams(("parallel", "parallel")),
        name="gla_mixer",
    )(pa, pa, pa, pa, log_g, onorm_g.reshape(1, GLA_DV))


def _t5_bucket(dist):
    max_exact = REL_BUCKETS // 2
    safe = np.maximum(dist, 1)
    large = max_exact + (np.log(safe / max_exact) / np.log(REL_MAX_DIST / max_exact)
                         * (REL_BUCKETS - max_exact)).astype(np.int64)
    large = np.minimum(large, REL_BUCKETS - 1)
    return np.where(dist < max_exact, dist, large).astype(np.int32)


ATT_B3, ATT_B2_FIRST, ATT_B2_REST, ATT_B1_FIRST, ATT_B1_REST = range(5)
B2_SLABS, B2_Q_ROWS = 4, 32
B1_SLABS, B1_Q_ROWS = 16, 8
MASKED_BUCKET = -1


def _attention_tables():
    far = 10 ** 6

    def slab_tokens(n_slabs, rows, first_pos, class_step):
        pos = first_pos + np.arange(rows)
        return (N_CLASS * pos[None, :] + class_step * np.arange(n_slabs)[:, None]).reshape(-1)

    specs = []
    a = np.arange(CLASS_LEN)
    specs.append((N_CLASS * a, np.concatenate([N_CLASS * a, np.full(CLASS_LEN, far)]), DIL_PATTERNS[2][0]))
    for first in (True, False):
        tq = slab_tokens(B2_SLABS, B2_Q_ROWS, 0, 4)
        tk = slab_tokens(B2_SLABS, 2 * B2_Q_ROWS, 0 if first else -B2_Q_ROWS, 4)
        specs.append((tq, tk, DIL_PATTERNS[1][0]))
    for first in (True, False):
        tq = slab_tokens(B1_SLABS, B1_Q_ROWS, 0, 1)
        tk = slab_tokens(B1_SLABS, 2 * B1_Q_ROWS, 0 if first else -B1_Q_ROWS, 1)
        specs.append((tq, tk, DIL_PATTERNS[0][0]))
    buckets = []
    for tq, tk, window in specs:
        dist = tq[:, None] - tk[None, :]
        in_band = (dist >= 0) & (dist <= window)
        buckets.append(np.where(in_band, _t5_bucket(np.clip(dist, 0, None)), MASKED_BUCKET))
    return np.stack(buckets).astype(np.int32)


def _bias_kernel(rb_ref, bk_ref, o_ref):
    bk = bk_ref[...]
    for h in range(DIL_HEADS):
        acc = jnp.full(bk.shape, -jnp.inf, F32)
        for bucket in range(REL_BUCKETS):
            acc = jnp.where(bk == bucket, rb_ref[bucket, h], acc)
        o_ref[h] = acc


def _bias_tables(rel_bias, buckets):
    nt, q, q2 = buckets.shape
    return pl.pallas_call(
        _bias_kernel,
        grid=(nt,),
        in_specs=[pl.BlockSpec(memory_space=pltpu.SMEM),
                  pl.BlockSpec((None, q, q2), lambda i: (i, 0, 0))],
        out_specs=pl.BlockSpec((DIL_HEADS, None, q, q2), lambda i: (0, i, 0, 0)),
        out_shape=jax.ShapeDtypeStruct((DIL_HEADS, nt, q, q2), F32),
        compiler_params=_cparams(("parallel",)),
        name="t5_bias_tables",
    )(rel_bias, buckets)


def _gather(ref, slabs):
    return jnp.concatenate([ref[pl.ds(s, n), :] for s, n in slabs], axis=0)


def _scatter(ref, slabs, val):
    off = 0
    for s, n in slabs:
        ref[pl.ds(s, n), :] = val[off:off + n]
        off += n


def _attn_kernel(q_ref, k_ref, v_ref, bias_ref, o_ref,
                 q32_s, k32_s, v32_s, acc_s, m_s, l_s):
    qb = DIL_BLOCK

    def run_group(blocks, init):
        olds = [None if init else (_gather(m_s, sl), _gather(l_s, sl), _gather(acc_s, sl))
                for *_, sl in blocks]
        logits = [lax.dot_general(q, kw, (((1,), (1,)), ((), ())), preferred_element_type=F32)
                  for q, kw, *_ in blocks]
        probs, stats = [], []
        for s, (_, _, _, tab, nk, _), old in zip(logits, blocks, olds):
            s = s + bias_ref[tab, :, 0:nk]
            m_blk = jnp.broadcast_to(jnp.max(s, axis=-1, keepdims=True), (qb, LANE))
            if init:
                m_new, alpha = m_blk, None
            else:
                m_new = jnp.maximum(old[0], m_blk)
                alpha = jnp.exp(old[0] - m_new)
            p = jnp.exp(s - jnp.concatenate([m_new] * (nk // LANE), axis=1))
            l_blk = jnp.broadcast_to(jnp.sum(p, axis=-1, keepdims=True), (qb, LANE))
            probs.append(p.astype(BF16))
            stats.append((m_new, alpha, l_blk))
        pvs = [jnp.dot(p, vw, preferred_element_type=F32) for p, (_, _, vw, *_) in zip(probs, blocks)]
        for pv, (m_new, alpha, l_blk), old, (*_, sl) in zip(pvs, stats, olds, blocks):
            _scatter(m_s, sl, m_new)
            _scatter(l_s, sl, l_blk if init else alpha * old[1] + l_blk)
            _scatter(acc_s, sl, pv if init else alpha * old[2] + pv)

    def b3_body(i, carry):
        blocks = []
        for jj in range(ATT_GROUP):
            r0 = pl.multiple_of((i * ATT_GROUP + jj) * CLASS_LEN, CLASS_LEN)
            rows = pl.ds(r0, CLASS_LEN)
            blocks.append((q_ref[rows, :], k_ref[rows, :], v_ref[rows, :], ATT_B3, CLASS_LEN,
                           [(r0, CLASS_LEN)]))
        run_group(blocks, init=True)
        return carry

    lax.fori_loop(0, N_CLASS // ATT_GROUP, b3_body, 0)

    def b2_body(kk, carry):
        q0 = pl.multiple_of(kk * B2_Q_ROWS, B2_Q_ROWS)
        k0 = pl.multiple_of(jnp.maximum(q0 - B2_Q_ROWS, 0), B2_Q_ROWS)
        tab = jnp.where(kk == 0, ATT_B2_FIRST, ATT_B2_REST)
        blocks = []
        for e in range(4):
            bases = [(4 * c + e) * CLASS_LEN for c in range(B2_SLABS)]
            q_slabs = [(pl.multiple_of(base + q0, B2_Q_ROWS), B2_Q_ROWS) for base in bases]
            k_slabs = [(pl.multiple_of(base + k0, B2_Q_ROWS), 2 * B2_Q_ROWS) for base in bases]
            blocks.append((_gather(q_ref, q_slabs), _gather(k_ref, k_slabs), _gather(v_ref, k_slabs),
                           tab, 2 * qb, q_slabs))
        run_group(blocks, init=False)
        return carry

    lax.fori_loop(0, CLASS_LEN // B2_Q_ROWS, b2_body, 0)

    q32_s[...] = q_ref[...].astype(F32)
    k32_s[...] = k_ref[...].astype(F32)
    v32_s[...] = v_ref[...].astype(F32)

    def b1_body(i, carry):
        blocks = []
        for jj in range(ATT_GROUP):
            kk = i * ATT_GROUP + jj
            q0 = pl.multiple_of(kk * B1_Q_ROWS, B1_Q_ROWS)
            k0 = pl.multiple_of(jnp.maximum(q0 - B1_Q_ROWS, 0), B1_Q_ROWS)
            tab = jnp.where(kk == 0, ATT_B1_FIRST, ATT_B1_REST)
            q_slabs = [(pl.multiple_of(u * CLASS_LEN + q0, B1_Q_ROWS), B1_Q_ROWS)
                       for u in range(B1_SLABS)]
            k_slabs = [(pl.multiple_of(u * CLASS_LEN + k0, B1_Q_ROWS), 2 * B1_Q_ROWS)
                       for u in range(B1_SLABS)]
            blocks.append((_gather(q32_s, q_slabs).astype(BF16), _gather(k32_s, k_slabs).astype(BF16),
                           _gather(v32_s, k_slabs).astype(BF16), tab, 2 * qb, q_slabs))
        run_group(blocks, init=False)
        return carry

    lax.fori_loop(0, CLASS_LEN // B1_Q_ROWS // ATT_GROUP, b1_body, 0)

    o_ref[...] = (acc_s[...] / l_s[...]).astype(o_ref.dtype)


def _attention(qkv_p, bias_tabs):
    s = SEQ
    nt = bias_tabs.shape[1]
    heads = DIL_HEADS
    return pl.pallas_call(
        _attn_kernel,
        grid=(BATCH, heads),
        in_specs=[
            pl.BlockSpec((s, DIL_DH), lambda b, h: (b, h)),
            pl.BlockSpec((s, DIL_DH), lambda b, h: (b, heads + h)),
            pl.BlockSpec((s, DIL_DH), lambda b, h: (b, 2 * heads + h)),
            pl.BlockSpec((None, nt, DIL_BLOCK, 2 * DIL_BLOCK), lambda b, h: (h, 0, 0, 0)),
        ],
        out_specs=pl.BlockSpec((s, DIL_DH), lambda b, h: (b, h)),
        out_shape=jax.ShapeDtypeStruct((M_TOK, DIL_W), BF16),
        scratch_shapes=[
            pltpu.VMEM((s, DIL_DH), F32),
            pltpu.VMEM((s, DIL_DH), F32),
            pltpu.VMEM((s, DIL_DH), F32),
            pltpu.VMEM((s, DIL_DH), F32),
            pltpu.VMEM((s, LANE), F32),
            pltpu.VMEM((s, LANE), F32),
        ],
        compiler_params=_cparams(("parallel", "parallel")),
        name="dilated_attention",
    )(qkv_p, qkv_p, qkv_p, bias_tabs)


def _to_class_order(a):
    w = a.shape[1]
    return a.reshape(BATCH, CLASS_LEN, N_CLASS, w).transpose(0, 2, 1, 3).reshape(M_TOK, w)


def _to_token_order(a):
    w = a.shape[1]
    return a.reshape(BATCH, N_CLASS, CLASS_LEN, w).transpose(0, 2, 1, 3).reshape(M_TOK, w)


def _outproj_kernel(oa_ref, ob_ref, x_ref, w_ref, g_ref, x1_ref, h2_ref):
    acc = jnp.dot(oa_ref[...], w_ref[0:VA_W, :], preferred_element_type=F32)
    acc = acc + jnp.dot(ob_ref[...], w_ref[VA_W:, :], preferred_element_type=F32)
    x1 = x_ref[...] + acc
    x1_ref[...] = x1
    ms = jnp.mean(x1 * x1, axis=-1, keepdims=True)
    h2_ref[...] = (x1 * lax.rsqrt(ms + RMS_EPS) * g_ref[...]).astype(h2_ref.dtype)


def _outproj(out_a, out_b, x, w_out, norm2_g):
    m, d = x.shape
    tm = TM_OUT
    row = lambda w: pl.BlockSpec((tm, w), lambda i: (i, 0))
    return pl.pallas_call(
        _outproj_kernel,
        grid=(m // tm,),
        in_specs=[row(VA_W), row(DIL_W), row(d),
                  pl.BlockSpec((d, d), lambda i: (0, 0)),
                  pl.BlockSpec((1, d), lambda i: (0, 0))],
        out_specs=[row(d), row(d)],
        out_shape=[jax.ShapeDtypeStruct((m, d), F32), jax.ShapeDtypeStruct((m, d), BF16)],
        compiler_params=_cparams(("parallel",)),
        name="outproj_norm2",
    )(out_a, out_b, x, w_out, norm2_g.reshape(1, d))


def _ffn_kernel(h_ref, x_ref, wg_ref, wu_ref, wd_ref, o_ref):
    j = pl.program_id(1)

    @pl.when(j == 0)
    def _():
        o_ref[...] = x_ref[...]

    h = h_ref[...]
    gate = jnp.dot(h, wg_ref[...], preferred_element_type=F32)
    up = jnp.dot(h, wu_ref[...], preferred_element_type=F32)
    act = (gate * jax.nn.sigmoid(gate) * up).astype(BF16)
    o_ref[...] += jnp.dot(act, wd_ref[...], preferred_element_type=F32)


def _ffn(h2, x1, w_gate, w_up, w_down):
    m, d = x1.shape
    f = w_gate.shape[1]
    tm, tf = TM_FFN, TF_FFN
    return pl.pallas_call(
        _ffn_kernel,
        grid=(m // tm, f // tf),
        in_specs=[pl.BlockSpec((tm, d), lambda i, j: (i, 0)),
                  pl.BlockSpec((tm, d), lambda i, j: (i, 0)),
                  pl.BlockSpec((d, tf), lambda i, j: (0, j)),
                  pl.BlockSpec((d, tf), lambda i, j: (0, j)),
                  pl.BlockSpec((tf, d), lambda i, j: (j, 0))],
        out_specs=pl.BlockSpec((tm, d), lambda i, j: (i, 0)),
        out_shape=jax.ShapeDtypeStruct((m, d), F32),
        compiler_params=_cparams(("parallel", "arbitrary")),
        name="swiglu_ffn",
    )(h2, x1, w_gate, w_up, w_down)


def kernel(x, norm1_g, w_in, gla_gate_w2, gla_gate_b, gla_onorm_g, q_norm_g, k_norm_g, rel_bias,
           w_out, norm2_g, w_gate, w_up, w_down):
    bsz, s, d = x.shape
    assert (bsz, s, d) == (BATCH, SEQ, D_MODEL)
    bias_tabs = _bias_tables(rel_bias.astype(F32), jnp.asarray(_attention_tables()))

    xf = x.reshape(M_TOK, D_MODEL)
    for l in range(DEPTH):
        w_gate1 = jnp.pad(w_in[l][:, GA_OFF:PB_OFF], ((0, 0), (0, GATE_PAD - GLA_GATE_RANK))).astype(BF16)
        w_gate2 = jnp.pad(gla_gate_w2[l], ((0, GATE_PAD - GLA_GATE_RANK), (0, 0))).astype(BF16)
        w_qkv = w_in[l][:, PB_OFF:].astype(BF16)
        qk_gain = jnp.concatenate([jnp.tile(q_norm_g[l], DIL_HEADS) * (DIL_DH ** -0.5),
                                   jnp.tile(k_norm_g[l], DIL_HEADS)]).reshape(1, 2 * DIL_W).astype(F32)

        h = _rmsnorm(xf, norm1_g[l])
        pa = _proj_plain(h, w_in, l)
        log_g = _gate(h, w_gate1, w_gate2, gla_gate_b[l].reshape(1, QA_W).astype(F32))
        out_a = _gla(pa, log_g, gla_onorm_g[l])

        qkv_p = _proj_qkv(_to_class_order(h), w_qkv, qk_gain)
        out_b = _to_token_order(_attention(qkv_p, bias_tabs))

        x1, h2 = _outproj(out_a, out_b, xf, w_out[l].astype(BF16), norm2_g[l])
        xf = _ffn(h2, x1, w_gate[l].astype(BF16), w_up[l].astype(BF16), w_down[l].astype(BF16))
    return xf.reshape(bsz, s, d)
```

```python
import math

import numpy as np
import jax
import jax.numpy as jnp
from jax import lax
from jax.experimental import pallas as pl
from jax.experimental.pallas import tpu as pltpu

D_MODEL = 2048
BATCH = 4
SEQ = 2048
DEPTH = 2
M_TOK = BATCH * SEQ

GLA_HEADS = 4
GLA_DV = 256
GLA_DK = 128
GLA_GATE_RANK = 16
GLA_GATE_TAU = 16.0
GLA_CHUNK = 64
DIL_HEADS = 8
DIL_DH = 128
DIL_PATTERNS = ((128, 1), (512, 4), (2048, 16))
DIL_BLOCK = 128
REL_BUCKETS = 32
REL_MAX_DIST = 2048
FFN_HIDDEN = 5632
RMS_EPS = 1e-6

QA_W = GLA_HEADS * GLA_DK
VA_W = GLA_HEADS * GLA_DV
DIL_W = DIL_HEADS * DIL_DH
PA_QA, PA_KA, PA_VA, PA_RA = 0, QA_W, 2 * QA_W, 2 * QA_W + VA_W
PA_W = PA_RA + VA_W
GA_OFF = PA_W
PB_OFF = GA_OFF + GLA_GATE_RANK
PB_W = 3 * DIL_W
LANE = 128
F32_ROWS = 8
GATE_PAD = LANE

N_CLASS = 16
CLASS_LEN = SEQ // N_CLASS

BF16 = jnp.bfloat16
F32 = jnp.float32

VMEM_LIMIT = 56 * 1024 * 1024

TM_NORM = 512
TM_PROJ, TN_PROJ = 1024, 1024
TM_GATE = 1024
TM_OUT = 256
TM_FFN, TF_FFN = 512, 512
GLA_GROUP = 4
ATT_GROUP = 8
LOG2E = math.log2(math.e)


def _cparams(sem):
    return pltpu.CompilerParams(dimension_semantics=sem, vmem_limit_bytes=VMEM_LIMIT)


def _rmsnorm_kernel(x_ref, g_ref, o_ref):
    x = x_ref[...]
    ms = jnp.mean(x * x, axis=-1, keepdims=True)
    o_ref[...] = (x * lax.rsqrt(ms + RMS_EPS) * g_ref[...]).astype(o_ref.dtype)


def _rmsnorm(x, g):
    m, d = x.shape
    return pl.pallas_call(
        _rmsnorm_kernel,
        grid=(m // TM_NORM,),
        in_specs=[pl.BlockSpec((TM_NORM, d), lambda i: (i, 0)),
                  pl.BlockSpec((1, d), lambda i: (0, 0))],
        out_specs=pl.BlockSpec((TM_NORM, d), lambda i: (i, 0)),
        out_shape=jax.ShapeDtypeStruct((m, d), BF16),
        compiler_params=_cparams(("parallel",)),
        name="rmsnorm",
    )(x, g.reshape(1, d))


def _proj_plain_kernel(h_ref, w_ref, o_ref):
    o_ref[...] = jnp.dot(h_ref[...], w_ref[...], preferred_element_type=F32).astype(o_ref.dtype)


def _proj_plain(h, w):
    m, k = h.shape
    n = w.shape[1]
    return pl.pallas_call(
        _proj_plain_kernel,
        grid=(n // TN_PROJ, m // TM_PROJ),
        in_specs=[pl.BlockSpec((TM_PROJ, k), lambda j, i: (i, 0)),
                  pl.BlockSpec((k, TN_PROJ), lambda j, i: (0, j))],
        out_specs=pl.BlockSpec((TM_PROJ, TN_PROJ), lambda j, i: (i, j)),
        out_shape=jax.ShapeDtypeStruct((m, n), BF16),
        compiler_params=_cparams(("parallel", "parallel")),
        name="proj_plain",
    )(h, w)


def _proj_qkv_kernel(h_ref, w_ref, g_ref, o_ref):
    acc = jnp.dot(h_ref[...], w_ref[...], preferred_element_type=F32)
    j = pl.program_id(1)

    @pl.when(j < 2)
    def _():
        for hh in range(acc.shape[1] // DIL_DH):
            cs = slice(hh * DIL_DH, (hh + 1) * DIL_DH)
            a = acc[:, cs]
            ms = jnp.mean(a * a, axis=-1, keepdims=True)
            o_ref[:, cs] = (a * lax.rsqrt(ms + RMS_EPS) * g_ref[:, cs]).astype(o_ref.dtype)

    @pl.when(j == 2)
    def _():
        o_ref[...] = acc.astype(o_ref.dtype)


def _proj_qkv(h, w, g):
    m, k = h.shape
    assert TN_PROJ == DIL_W
    return pl.pallas_call(
        _proj_qkv_kernel,
        grid=(m // TM_PROJ, PB_W // TN_PROJ),
        in_specs=[pl.BlockSpec((TM_PROJ, k), lambda i, j: (i, 0)),
                  pl.BlockSpec((k, TN_PROJ), lambda i, j: (0, j)),
                  pl.BlockSpec((1, TN_PROJ), lambda i, j: (0, jnp.minimum(j, 1)))],
        out_specs=pl.BlockSpec((TM_PROJ, TN_PROJ), lambda i, j: (i, j)),
        out_shape=jax.ShapeDtypeStruct((m, PB_W), BF16),
        compiler_params=_cparams(("parallel", "arbitrary")),
        name="proj_qkv",
    )(h, w, g)


def _gate_kernel(h_ref, w1_ref, w2_ref, b_ref, o_ref):
    ga = jnp.dot(h_ref[...], w1_ref[...], preferred_element_type=F32)
    pre = jnp.dot(ga.astype(BF16), w2_ref[...], preferred_element_type=F32) + b_ref[...]
    o_ref[...] = jax.nn.log_sigmoid(pre) * (1.0 / GLA_GATE_TAU)


def _gate(h, w1, w2, b):
    m, k = h.shape
    n = w2.shape[1]
    return pl.pallas_call(
        _gate_kernel,
        grid=(m // TM_GATE,),
        in_specs=[pl.BlockSpec((TM_GATE, k), lambda i: (i, 0)),
                  pl.BlockSpec((k, GATE_PAD), lambda i: (0, 0)),
                  pl.BlockSpec((GATE_PAD, n), lambda i: (0, 0)),
                  pl.BlockSpec((1, n), lambda i: (0, 0))],
        out_specs=pl.BlockSpec((TM_GATE, n), lambda i: (i, 0)),
        out_shape=jax.ShapeDtypeStruct((m, n), F32),
        compiler_params=_cparams(("parallel",)),
        name="gla_gate",
    )(h, w1, w2, b)


def _gla_kernel(q_ref, k_ref, v_ref, r_ref, lg_ref, gn_ref, o_ref,
                qt_s, kt_s, kd_s, dec_s, st_s, o_s):
    s_len, c, n_chunks = SEQ, GLA_CHUNK, SEQ // GLA_CHUNK
    b = lg_ref[...]
    row = lax.broadcasted_iota(jnp.int32, b.shape, 0) & (c - 1)
    shift = 1
    while shift < c:
        b = b + jnp.where(row >= shift, pltpu.roll(b, shift, axis=0), 0.0)
        shift *= 2
    b3 = b.reshape(n_chunks, c, GLA_DK)
    b_last = jnp.broadcast_to(b3[:, c - 1:c, :], b3.shape).reshape(s_len, GLA_DK)
    q = q_ref[...].astype(F32) * (GLA_DK ** -0.5)
    k = k_ref[...].astype(F32)
    qt_s[...] = (q * jnp.exp(b)).astype(BF16)
    kt_s[...] = (k * jnp.exp(-b)).astype(BF16)
    kd_s[...] = (k * jnp.exp(b_last - b)).astype(BF16)
    dec_s[...] = jnp.exp(b_last)
    st_s[...] = jnp.zeros_like(st_s)

    ri = lax.broadcasted_iota(jnp.int32, (c, c), 0)
    ci = lax.broadcasted_iota(jnp.int32, (c, c), 1)
    causal = ri >= ci

    nt_dims = (((1,), (1,)), ((), ()))
    tn_dims = (((0,), (0,)), ((), ()))

    def chunk_group(gi, carry):
        starts = [pl.multiple_of((gi * GLA_GROUP + j) * c, c) for j in range(GLA_GROUP)]
        rows = [pl.ds(r0, c) for r0 in starts]
        qts = [qt_s[r, :] for r in rows]
        vs = [v_ref[r, :] for r in rows]
        attn = [lax.dot_general(qt, kt_s[r, :], nt_dims, preferred_element_type=F32)
                for qt, r in zip(qts, rows)]
        cs_t = [lax.dot_general(v, kd_s[r, :], tn_dims, preferred_element_type=F32)
                for v, r in zip(vs, rows)]
        attn = [jnp.where(causal, a, 0.0).astype(BF16) for a in attn]
        o_intra = [jnp.dot(a, v, preferred_element_type=F32) for a, v in zip(attn, vs)]
        st = st_s[...]
        states = []
        for r0, cs in zip(starts, cs_t):
            states.append(st.astype(BF16))
            st = st * dec_s[pl.ds(r0, F32_ROWS), :][0:1, :] + cs
        st_s[...] = st
        o_inter = [lax.dot_general(qt, sb, nt_dims, preferred_element_type=F32)
                   for qt, sb in zip(qts, states)]
        for r, a, b_ in zip(rows, o_intra, o_inter):
            o_s[r, :] = a + b_
        return carry

    lax.fori_loop(0, n_chunks // GLA_GROUP, chunk_group, 0)

    o = o_s[...]
    ms = jnp.mean(o * o, axis=-1, keepdims=True)
    y = o * lax.rsqrt(ms + RMS_EPS) * gn_ref[...]
    r = r_ref[...].astype(F32)
    o_ref[...] = (y * (r * jax.nn.sigmoid(r))).astype(o_ref.dtype)


def _gla(pa, log_g, onorm_g):
    s = SEQ
    return pl.pallas_call(
        _gla_kernel,
        grid=(BATCH, GLA_HEADS),
        in_specs=[
            pl.BlockSpec((s, GLA_DK), lambda b, h: (b, PA_QA // GLA_DK + h)),
            pl.BlockSpec((s, GLA_DK), lambda b, h: (b, PA_KA // GLA_DK + h)),
            pl.BlockSpec((s, GLA_DV), lambda b, h: (b, PA_VA // GLA_DV + h)),
            pl.BlockSpec((s, GLA_DV), lambda b, h: (b, PA_RA // GLA_DV + h)),
            pl.BlockSpec((s, GLA_DK), lambda b, h: (b, h)),
            pl.BlockSpec((1, GLA_DV), lambda b, h: (0, 0)),
        ],
        out_specs=pl.BlockSpec((s, GLA_DV), lambda b, h: (b, h)),
        out_shape=jax.ShapeDtypeStruct((M_TOK, VA_W), BF16),
        scratch_shapes=[
            pltpu.VMEM((s, GLA_DK), BF16),
            pltpu.VMEM((s, GLA_DK), BF16),
            pltpu.VMEM((s, GLA_DK), BF16),
            pltpu.VMEM((s, GLA_DK), F32),
            pltpu.VMEM((GLA_DV, GLA_DK), F32),
            pltpu.VMEM((s, GLA_DV), F32),
        ],
        compiler_params=_cparams(("parallel", "parallel")),
        name="gla_mixer",
    )(pa, pa, pa, pa, log_g, onorm_g.reshape(1, GLA_DV))


def _t5_bucket(dist):
    max_exact = REL_BUCKETS // 2
    safe = np.maximum(dist, 1)
    large = max_exact + (np.log(safe / max_exact) / np.log(REL_MAX_DIST / max_exact)
                         * (REL_BUCKETS - max_exact)).astype(np.int64)
    large = np.minimum(large, REL_BUCKETS - 1)
    return np.where(dist < max_exact, dist, large).astype(np.int32)


ATT_B3, ATT_B2_FIRST, ATT_B2_REST, ATT_B1_FIRST, ATT_B1_REST = range(5)
B2_SLABS, B2_Q_ROWS = 4, 32
B1_SLABS, B1_Q_ROWS = 16, 8
MASKED_BUCKET = -1


def _attention_tables():
    far = 10 ** 6

    def slab_tokens(n_slabs, rows, first_pos, class_step):
        pos = first_pos + np.arange(rows)
        return (N_CLASS * pos[None, :] + class_step * np.arange(n_slabs)[:, None]).reshape(-1)

    specs = []
    a = np.arange(CLASS_LEN)
    specs.append((N_CLASS * a, np.concatenate([N_CLASS * a, np.full(CLASS_LEN, far)]), DIL_PATTERNS[2][0]))
    for first in (True, False):
        tq = slab_tokens(B2_SLABS, B2_Q_ROWS, 0, 4)
        tk = slab_tokens(B2_SLABS, 2 * B2_Q_ROWS, 0 if first else -B2_Q_ROWS, 4)
        specs.append((tq, tk, DIL_PATTERNS[1][0]))
    for first in (True, False):
        tq = slab_tokens(B1_SLABS, B1_Q_ROWS, 0, 1)
        tk = slab_tokens(B1_SLABS, 2 * B1_Q_ROWS, 0 if first else -B1_Q_ROWS, 1)
        specs.append((tq, tk, DIL_PATTERNS[0][0]))
    buckets = []
    for tq, tk, window in specs:
        dist = tq[:, None] - tk[None, :]
        in_band = (dist >= 0) & (dist <= window)
        buckets.append(np.where(in_band, _t5_bucket(np.clip(dist, 0, None)), MASKED_BUCKET))
    return np.stack(buckets).astype(np.int32)


def _bias_kernel(rb_ref, bk_ref, o_ref):
    bk = bk_ref[...]
    for h in range(DIL_HEADS):
        acc = jnp.full(bk.shape, -jnp.inf, F32)
        for bucket in range(REL_BUCKETS):
            acc = jnp.where(bk == bucket, rb_ref[bucket, h] * LOG2E, acc)
        o_ref[h] = acc


def _bias_tables(rel_bias, buckets):
    nt, q, q2 = buckets.shape
    return pl.pallas_call(
        _bias_kernel,
        grid=(nt,),
        in_specs=[pl.BlockSpec(memory_space=pltpu.SMEM),
                  pl.BlockSpec((None, q, q2), lambda i: (i, 0, 0))],
        out_specs=pl.BlockSpec((DIL_HEADS, None, q, q2), lambda i: (0, i, 0, 0)),
        out_shape=jax.ShapeDtypeStruct((DIL_HEADS, nt, q, q2), F32),
        compiler_params=_cparams(("parallel",)),
        name="t5_bias_tables",
    )(rel_bias, buckets)


def _gather(ref, slabs):
    return jnp.concatenate([ref[pl.ds(s, n), :] for s, n in slabs], axis=0)


def _scatter(ref, slabs, val):
    off = 0
    for s, n in slabs:
        ref[pl.ds(s, n), :] = val[off:off + n]
        off += n


def _attn_kernel(q_ref, k_ref, v_ref, bias_ref, o_ref,
                 q32_s, k32_s, v32_s, acc_s, m_s, l_s):
    qb = DIL_BLOCK

    def run_group(blocks, init):
        olds = [None if init else (_gather(m_s, sl), _gather(l_s, sl), _gather(acc_s, sl))
                for *_, sl in blocks]
        logits = [lax.dot_general(q, kw, (((1,), (1,)), ((), ())), preferred_element_type=F32)
                  for q, kw, *_ in blocks]
        probs, stats = [], []
        for s, (_, _, _, tab, nk, _), old in zip(logits, blocks, olds):
            s = s + bias_ref[tab, :, 0:nk]
            m_blk = jnp.broadcast_to(jnp.max(s, axis=-1, keepdims=True), (qb, LANE))
            if init:
                m_new, alpha = m_blk, None
            else:
                m_new = jnp.maximum(old[0], m_blk)
                alpha = jnp.exp2(old[0] - m_new)
            p = jnp.exp2(s - jnp.concatenate([m_new] * (nk // LANE), axis=1))
            l_blk = jnp.broadcast_to(jnp.sum(p, axis=-1, keepdims=True), (qb, LANE))
            probs.append(p.astype(BF16))
            stats.append((m_new, alpha, l_blk))
        pvs = [jnp.dot(p, vw, preferred_element_type=F32) for p, (_, _, vw, *_) in zip(probs, blocks)]
        for pv, (m_new, alpha, l_blk), old, (*_, sl) in zip(pvs, stats, olds, blocks):
            _scatter(m_s, sl, m_new)
            _scatter(l_s, sl, l_blk if init else alpha * old[1] + l_blk)
            _scatter(acc_s, sl, pv if init else alpha * old[2] + pv)

    def b3_body(i, carry):
        blocks = []
        for jj in range(ATT_GROUP):
            r0 = pl.multiple_of((i * ATT_GROUP + jj) * CLASS_LEN, CLASS_LEN)
            rows = pl.ds(r0, CLASS_LEN)
            blocks.append((q_ref[rows, :], k_ref[rows, :], v_ref[rows, :], ATT_B3, CLASS_LEN,
                           [(r0, CLASS_LEN)]))
        run_group(blocks, init=True)
        return carry

    lax.fori_loop(0, N_CLASS // ATT_GROUP, b3_body, 0)

    n_res = DIL_PATTERNS[1][1]
    assert ATT_GROUP % n_res == 0

    def b2_body(i, carry):
        blocks = []
        for jj in range(ATT_GROUP):
            e = jj % n_res
            kk = i * (ATT_GROUP // n_res) + jj // n_res
            q0 = pl.multiple_of(kk * B2_Q_ROWS, B2_Q_ROWS)
            k0 = pl.multiple_of(jnp.maximum(q0 - B2_Q_ROWS, 0), B2_Q_ROWS)
            tab = jnp.where(kk == 0, ATT_B2_FIRST, ATT_B2_REST)
            bases = [(n_res * c + e) * CLASS_LEN for c in range(B2_SLABS)]
            q_slabs = [(pl.multiple_of(base + q0, B2_Q_ROWS), B2_Q_ROWS) for base in bases]
            k_slabs = [(pl.multiple_of(base + k0, B2_Q_ROWS), 2 * B2_Q_ROWS) for base in bases]
            blocks.append((_gather(q_ref, q_slabs), _gather(k_ref, k_slabs), _gather(v_ref, k_slabs),
                           tab, 2 * qb, q_slabs))
        run_group(blocks, init=False)
        return carry

    lax.fori_loop(0, n_res * (CLASS_LEN // B2_Q_ROWS) // ATT_GROUP, b2_body, 0)

    q32_s[...] = q_ref[...].astype(F32)
    k32_s[...] = k_ref[...].astype(F32)
    v32_s[...] = v_ref[...].astype(F32)

    def b1_body(i, carry):
        blocks = []
        for jj in range(ATT_GROUP):
            kk = i * ATT_GROUP + jj
            q0 = pl.multiple_of(kk * B1_Q_ROWS, B1_Q_ROWS)
            k0 = pl.multiple_of(jnp.maximum(q0 - B1_Q_ROWS, 0), B1_Q_ROWS)
            tab = jnp.where(kk == 0, ATT_B1_FIRST, ATT_B1_REST)
            q_slabs = [(pl.multiple_of(u * CLASS_LEN + q0, B1_Q_ROWS), B1_Q_ROWS)
                       for u in range(B1_SLABS)]
            k_slabs = [(pl.multiple_of(u * CLASS_LEN + k0, B1_Q_ROWS), 2 * B1_Q_ROWS)
                       for u in range(B1_SLABS)]
            blocks.append((_gather(q32_s, q_slabs).astype(BF16), _gather(k32_s, k_slabs).astype(BF16),
                           _gather(v32_s, k_slabs).astype(BF16), tab, 2 * qb, q_slabs))
        run_group(blocks, init=False)
        return carry

    lax.fori_loop(0, CLASS_LEN // B1_Q_ROWS // ATT_GROUP, b1_body, 0)

    o_ref[...] = (acc_s[...] / l_s[...]).astype(o_ref.dtype)


def _attention(qkv_p, bias_tabs):
    s = SEQ
    nt = bias_tabs.shape[1]
    heads = DIL_HEADS
    return pl.pallas_call(
        _attn_kernel,
        grid=(BATCH, heads),
        in_specs=[
            pl.BlockSpec((s, DIL_DH), lambda b, h: (b, h)),
            pl.BlockSpec((s, DIL_DH), lambda b, h: (b, heads + h)),
            pl.BlockSpec((s, DIL_DH), lambda b, h: (b, 2 * heads + h)),
            pl.BlockSpec((None, nt, DIL_BLOCK, 2 * DIL_BLOCK), lambda b, h: (h, 0, 0, 0)),
        ],
        out_specs=pl.BlockSpec((s, DIL_DH), lambda b, h: (b, h)),
        out_shape=jax.ShapeDtypeStruct((M_TOK, DIL_W), BF16),
        scratch_shapes=[
            pltpu.VMEM((s, DIL_DH), F32),
            pltpu.VMEM((s, DIL_DH), F32),
            pltpu.VMEM((s, DIL_DH), F32),
            pltpu.VMEM((s, DIL_DH), F32),
            pltpu.VMEM((s, LANE), F32),
            pltpu.VMEM((s, LANE), F32),
        ],
        compiler_params=_cparams(("parallel", "parallel")),
        name="dilated_attention",
    )(qkv_p, qkv_p, qkv_p, bias_tabs)


def _to_class_order(a):
    w = a.shape[1]
    return a.reshape(BATCH, CLASS_LEN, N_CLASS, w).transpose(0, 2, 1, 3).reshape(M_TOK, w)


def _to_token_order(a):
    w = a.shape[1]
    return a.reshape(BATCH, N_CLASS, CLASS_LEN, w).transpose(0, 2, 1, 3).reshape(M_TOK, w)


def _outproj_kernel(oa_ref, ob_ref, x_ref, w_ref, g_ref, x1_ref, h2_ref):
    acc = jnp.dot(oa_ref[...], w_ref[0:VA_W, :], preferred_element_type=F32)
    acc = acc + jnp.dot(ob_ref[...], w_ref[VA_W:, :], preferred_element_type=F32)
    x1 = x_ref[...] + acc
    x1_ref[...] = x1
    ms = jnp.mean(x1 * x1, axis=-1, keepdims=True)
    h2_ref[...] = (x1 * lax.rsqrt(ms + RMS_EPS) * g_ref[...]).astype(h2_ref.dtype)


def _outproj(out_a, out_b, x, w_out, layer, norm2_g):
    m, d = x.shape
    tm = TM_OUT
    row = lambda w: pl.BlockSpec((tm, w), lambda i: (i, 0))
    return pl.pallas_call(
        _outproj_kernel,
        grid=(m // tm,),
        in_specs=[row(VA_W), row(DIL_W), row(d),
                  pl.BlockSpec((None, d, d), lambda i: (layer, 0, 0)),
                  pl.BlockSpec((1, d), lambda i: (0, 0))],
        out_specs=[row(d), row(d)],
        out_shape=[jax.ShapeDtypeStruct((m, d), F32), jax.ShapeDtypeStruct((m, d), BF16)],
        compiler_params=_cparams(("parallel",)),
        name="outproj_norm2",
    )(out_a, out_b, x, w_out, norm2_g.reshape(1, d))


def _ffn_kernel(h_ref, x_ref, wg_ref, wu_ref, wd_ref, o_ref):
    j = pl.program_id(1)

    @pl.when(j == 0)
    def _():
        o_ref[...] = x_ref[...]

    h = h_ref[...]
    gate = jnp.dot(h, wg_ref[...], preferred_element_type=F32)
    up = jnp.dot(h, wu_ref[...], preferred_element_type=F32)
    act = (gate * jax.nn.sigmoid(gate) * up).astype(BF16)
    o_ref[...] += jnp.dot(act, wd_ref[...], preferred_element_type=F32)


def _ffn(h2, x1, w_gate, w_up, w_down, layer):
    m, d = x1.shape
    f = w_gate.shape[2]
    tm, tf = TM_FFN, TF_FFN
    return pl.pallas_call(
        _ffn_kernel,
        grid=(m // tm, f // tf),
        in_specs=[pl.BlockSpec((tm, d), lambda i, j: (i, 0)),
                  pl.BlockSpec((tm, d), lambda i, j: (i, 0)),
                  pl.BlockSpec((None, d, tf), lambda i, j: (layer, 0, j)),
                  pl.BlockSpec((None, d, tf), lambda i, j: (layer, 0, j)),
                  pl.BlockSpec((None, tf, d), lambda i, j: (layer, j, 0))],
        out_specs=pl.BlockSpec((tm, d), lambda i, j: (i, 0)),
        out_shape=jax.ShapeDtypeStruct((m, d), F32),
        compiler_params=_cparams(("parallel", "arbitrary")),
        name="swiglu_ffn",
    )(h2, x1, w_gate, w_up, w_down)


def kernel(x, norm1_g, w_in, gla_gate_w2, gla_gate_b, gla_onorm_g, q_norm_g, k_norm_g, rel_bias,
           w_out, norm2_g, w_gate, w_up, w_down):
    bsz, s, d = x.shape
    assert (bsz, s, d) == (BATCH, SEQ, D_MODEL)
    bias_tabs = _bias_tables(rel_bias.astype(F32), jnp.asarray(_attention_tables()))
    w_out_b, w_gate_b, w_up_b, w_down_b = (w.astype(BF16) for w in (w_out, w_gate, w_up, w_down))

    xf = x.reshape(M_TOK, D_MODEL)
    for l in range(DEPTH):
        w_plain = w_in[l][:, :PA_W].astype(BF16)
        w_gate1 = jnp.pad(w_in[l][:, GA_OFF:PB_OFF], ((0, 0), (0, GATE_PAD - GLA_GATE_RANK))).astype(BF16)
        w_gate2 = jnp.pad(gla_gate_w2[l], ((0, GATE_PAD - GLA_GATE_RANK), (0, 0))).astype(BF16)
        w_qkv = w_in[l][:, PB_OFF:].astype(BF16)
        qk_gain = jnp.concatenate([jnp.tile(q_norm_g[l], DIL_HEADS) * (DIL_DH ** -0.5 * LOG2E),
                                   jnp.tile(k_norm_g[l], DIL_HEADS)]).reshape(1, 2 * DIL_W).astype(F32)

        h = _rmsnorm(xf, norm1_g[l])
        pa = _proj_plain(h, w_plain)
        log_g = _gate(h, w_gate1, w_gate2, gla_gate_b[l].reshape(1, QA_W).astype(F32))
        out_a = _gla(pa, log_g, gla_onorm_g[l])

        qkv_p = _proj_qkv(_to_class_order(h), w_qkv, qk_gain)
        out_b = _to_token_order(_attention(qkv_p, bias_tabs))

        x1, h2 = _outproj(out_a, out_b, xf, w_out_b, l, norm2_g[l])
        xf = _ffn(h2, x1, w_gate_b, w_up_b, w_down_b, l)
    return xf.reshape(bsz, s, d)
```

```python
import math

import numpy as np
import jax
import jax.numpy as jnp
from jax import lax
from jax.experimental import pallas as pl
from jax.experimental.pallas import tpu as pltpu

D_MODEL = 2048
BATCH = 4
SEQ = 2048
DEPTH = 2
M_TOK = BATCH * SEQ

GLA_HEADS = 4
GLA_DV = 256
GLA_DK = 128
GLA_GATE_RANK = 16
GLA_GATE_TAU = 16.0
GLA_CHUNK = 64
DIL_HEADS = 8
DIL_DH = 128
DIL_PATTERNS = ((128, 1), (512, 4), (2048, 16))
DIL_BLOCK = 128
REL_BUCKETS = 32
REL_MAX_DIST = 2048
FFN_HIDDEN = 5632
RMS_EPS = 1e-6

QA_W = GLA_HEADS * GLA_DK
VA_W = GLA_HEADS * GLA_DV
DIL_W = DIL_HEADS * DIL_DH
PA_QA, PA_KA, PA_VA, PA_RA = 0, QA_W, 2 * QA_W, 2 * QA_W + VA_W
PA_W = PA_RA + VA_W
GA_OFF = PA_W
PB_OFF = GA_OFF + GLA_GATE_RANK
PB_W = 3 * DIL_W
LANE = 128
F32_ROWS = 8
GATE_PAD = LANE

N_CLASS = 16
CLASS_LEN = SEQ // N_CLASS

BF16 = jnp.bfloat16
F32 = jnp.float32

VMEM_LIMIT = 56 * 1024 * 1024

TM_NORM = 512
TM_PROJ, TN_PROJ = 1024, 1024
TM_GATE = 1024
TM_OUT = 256
TM_FFN, TF_FFN = 1024, 256
GLA_GROUP = 4
ATT_GROUP = 8
LOG2E = math.log2(math.e)


def _cparams(sem):
    return pltpu.CompilerParams(dimension_semantics=sem, vmem_limit_bytes=VMEM_LIMIT)


def _rmsnorm_kernel(x_ref, g_ref, o_ref):
    x = x_ref[...]
    ms = jnp.mean(x * x, axis=-1, keepdims=True)
    o_ref[...] = (x * lax.rsqrt(ms + RMS_EPS) * g_ref[...]).astype(o_ref.dtype)


def _rmsnorm(x, g):
    m, d = x.shape
    return pl.pallas_call(
        _rmsnorm_kernel,
        grid=(m // TM_NORM,),
        in_specs=[pl.BlockSpec((TM_NORM, d), lambda i: (i, 0)),
                  pl.BlockSpec((1, d), lambda i: (0, 0))],
        out_specs=pl.BlockSpec((TM_NORM, d), lambda i: (i, 0)),
        out_shape=jax.ShapeDtypeStruct((m, d), BF16),
        compiler_params=_cparams(("parallel",)),
        name="rmsnorm",
    )(x, g.reshape(1, d))


def _proj_plain_kernel(h_ref, w_ref, o_ref):
    o_ref[...] = jnp.dot(h_ref[...], w_ref[...], preferred_element_type=F32).astype(o_ref.dtype)


def _proj_plain(h, w):
    m, k = h.shape
    n = w.shape[1]
    return pl.pallas_call(
        _proj_plain_kernel,
        grid=(n // TN_PROJ, m // TM_PROJ),
        in_specs=[pl.BlockSpec((TM_PROJ, k), lambda j, i: (i, 0)),
                  pl.BlockSpec((k, TN_PROJ), lambda j, i: (0, j))],
        out_specs=pl.BlockSpec((TM_PROJ, TN_PROJ), lambda j, i: (i, j)),
        out_shape=jax.ShapeDtypeStruct((m, n), BF16),
        compiler_params=_cparams(("parallel", "parallel")),
        name="proj_plain",
    )(h, w)


def _proj_qkv_kernel(h_ref, w_ref, g_ref, o_ref):
    acc = jnp.dot(h_ref[...], w_ref[...], preferred_element_type=F32)
    j = pl.program_id(1)

    @pl.when(j < 2)
    def _():
        for hh in range(acc.shape[1] // DIL_DH):
            cs = slice(hh * DIL_DH, (hh + 1) * DIL_DH)
            a = acc[:, cs]
            ms = jnp.mean(a * a, axis=-1, keepdims=True)
            o_ref[:, cs] = (a * lax.rsqrt(ms + RMS_EPS) * g_ref[:, cs]).astype(o_ref.dtype)

    @pl.when(j == 2)
    def _():
        o_ref[...] = acc.astype(o_ref.dtype)


def _proj_qkv(h, w, g):
    m, k = h.shape
    assert TN_PROJ == DIL_W
    return pl.pallas_call(
        _proj_qkv_kernel,
        grid=(m // TM_PROJ, PB_W // TN_PROJ),
        in_specs=[pl.BlockSpec((TM_PROJ, k), lambda i, j: (i, 0)),
                  pl.BlockSpec((k, TN_PROJ), lambda i, j: (0, j)),
                  pl.BlockSpec((1, TN_PROJ), lambda i, j: (0, jnp.minimum(j, 1)))],
        out_specs=pl.BlockSpec((TM_PROJ, TN_PROJ), lambda i, j: (i, j)),
        out_shape=jax.ShapeDtypeStruct((m, PB_W), BF16),
        compiler_params=_cparams(("parallel", "arbitrary")),
        name="proj_qkv",
    )(h, w, g)


def _gate_kernel(h_ref, w1_ref, w2_ref, b_ref, o_ref):
    ga = jnp.dot(h_ref[...], w1_ref[...], preferred_element_type=F32)
    pre = jnp.dot(ga.astype(BF16), w2_ref[...], preferred_element_type=F32) + b_ref[...]
    o_ref[...] = jax.nn.log_sigmoid(pre) * (1.0 / GLA_GATE_TAU)


def _gate(h, w1, w2, b):
    m, k = h.shape
    n = w2.shape[1]
    return pl.pallas_call(
        _gate_kernel,
        grid=(m // TM_GATE,),
        in_specs=[pl.BlockSpec((TM_GATE, k), lambda i: (i, 0)),
                  pl.BlockSpec((k, GATE_PAD), lambda i: (0, 0)),
                  pl.BlockSpec((GATE_PAD, n), lambda i: (0, 0)),
                  pl.BlockSpec((1, n), lambda i: (0, 0))],
        out_specs=pl.BlockSpec((TM_GATE, n), lambda i: (i, 0)),
        out_shape=jax.ShapeDtypeStruct((m, n), F32),
        compiler_params=_cparams(("parallel",)),
        name="gla_gate",
    )(h, w1, w2, b)


def _gla_kernel(q_ref, k_ref, v_ref, r_ref, lg_ref, gn_ref, o_ref,
                qt_s, kt_s, kd_s, dec_s, st_s, o_s):
    s_len, c, n_chunks = SEQ, GLA_CHUNK, SEQ // GLA_CHUNK
    b = lg_ref[...]
    row = lax.broadcasted_iota(jnp.int32, b.shape, 0) & (c - 1)
    shift = 1
    while shift < c:
        b = b + jnp.where(row >= shift, pltpu.roll(b, shift, axis=0), 0.0)
        shift *= 2
    b3 = b.reshape(n_chunks, c, GLA_DK)
    b_last = jnp.broadcast_to(b3[:, c - 1:c, :], b3.shape).reshape(s_len, GLA_DK)
    q = q_ref[...].astype(F32) * (GLA_DK ** -0.5)
    k = k_ref[...].astype(F32)
    qt_s[...] = (q * jnp.exp(b)).astype(BF16)
    kt_s[...] = (k * jnp.exp(-b)).astype(BF16)
    kd_s[...] = (k * jnp.exp(b_last - b)).astype(BF16)
    dec_s[...] = jnp.exp(b_last)
    st_s[...] = jnp.zeros_like(st_s)

    ri = lax.broadcasted_iota(jnp.int32, (c, c), 0)
    ci = lax.broadcasted_iota(jnp.int32, (c, c), 1)
    causal = ri >= ci

    nt_dims = (((1,), (1,)), ((), ()))
    tn_dims = (((0,), (0,)), ((), ()))

    def chunk_group(gi, carry):
        starts = [pl.multiple_of((gi * GLA_GROUP + j) * c, c) for j in range(GLA_GROUP)]
        rows = [pl.ds(r0, c) for r0 in starts]
        qts = [qt_s[r, :] for r in rows]
        vs = [v_ref[r, :] for r in rows]
        attn = [lax.dot_general(qt, kt_s[r, :], nt_dims, preferred_element_type=F32)
                for qt, r in zip(qts, rows)]
        cs_t = [lax.dot_general(v, kd_s[r, :], tn_dims, preferred_element_type=F32)
                for v, r in zip(vs, rows)]
        attn = [jnp.where(causal, a, 0.0).astype(BF16) for a in attn]
        o_intra = [jnp.dot(a, v, preferred_element_type=F32) for a, v in zip(attn, vs)]
        st = st_s[...]
        states = []
        for r0, cs in zip(starts, cs_t):
            states.append(st.astype(BF16))
            st = st * dec_s[pl.ds(r0, F32_ROWS), :][0:1, :] + cs
        st_s[...] = st
        o_inter = [lax.dot_general(qt, sb, nt_dims, preferred_element_type=F32)
                   for qt, sb in zip(qts, states)]
        for r, a, b_ in zip(rows, o_intra, o_inter):
            o_s[r, :] = a + b_
        return carry

    lax.fori_loop(0, n_chunks // GLA_GROUP, chunk_group, 0)

    o = o_s[...]
    ms = jnp.mean(o * o, axis=-1, keepdims=True)
    y = o * lax.rsqrt(ms + RMS_EPS) * gn_ref[...]
    r = r_ref[...].astype(F32)
    o_ref[...] = (y * (r * jax.nn.sigmoid(r))).astype(o_ref.dtype)


def _gla(pa, log_g, onorm_g):
    s = SEQ
    return pl.pallas_call(
        _gla_kernel,
        grid=(BATCH, GLA_HEADS),
        in_specs=[
            pl.BlockSpec((s, GLA_DK), lambda b, h: (b, PA_QA // GLA_DK + h)),
            pl.BlockSpec((s, GLA_DK), lambda b, h: (b, PA_KA // GLA_DK + h)),
            pl.BlockSpec((s, GLA_DV), lambda b, h: (b, PA_VA // GLA_DV + h)),
            pl.BlockSpec((s, GLA_DV), lambda b, h: (b, PA_RA // GLA_DV + h)),
            pl.BlockSpec((s, GLA_DK), lambda b, h: (b, h)),
            pl.BlockSpec((1, GLA_DV), lambda b, h: (0, 0)),
        ],
        out_specs=pl.BlockSpec((s, GLA_DV), lambda b, h: (b, h)),
        out_shape=jax.ShapeDtypeStruct((M_TOK, VA_W), BF16),
        scratch_shapes=[
            pltpu.VMEM((s, GLA_DK), BF16),
            pltpu.VMEM((s, GLA_DK), BF16),
            pltpu.VMEM((s, GLA_DK), BF16),
            pltpu.VMEM((s, GLA_DK), F32),
            pltpu.VMEM((GLA_DV, GLA_DK), F32),
            pltpu.VMEM((s, GLA_DV), F32),
        ],
        compiler_params=_cparams(("parallel", "parallel")),
        name="gla_mixer",
    )(pa, pa, pa, pa, log_g, onorm_g.reshape(1, GLA_DV))


def _t5_bucket(dist):
    max_exact = REL_BUCKETS // 2
    safe = np.maximum(dist, 1)
    large = max_exact + (np.log(safe / max_exact) / np.log(REL_MAX_DIST / max_exact)
                         * (REL_BUCKETS - max_exact)).astype(np.int64)
    large = np.minimum(large, REL_BUCKETS - 1)
    return np.where(dist < max_exact, dist, large).astype(np.int32)


ATT_B3, ATT_B2_FIRST, ATT_B2_REST, ATT_B1_FIRST, ATT_B1_REST = range(5)
B2_SLABS, B2_Q_ROWS = 4, 32
B1_SLABS, B1_Q_ROWS = 16, 8
MASKED_BUCKET = -1


def _attention_tables():
    far = 10 ** 6

    def slab_tokens(n_slabs, rows, first_pos, class_step):
        pos = first_pos + np.arange(rows)
        return (N_CLASS * pos[None, :] + class_step * np.arange(n_slabs)[:, None]).reshape(-1)

    specs = []
    a = np.arange(CLASS_LEN)
    specs.append((N_CLASS * a, np.concatenate([N_CLASS * a, np.full(CLASS_LEN, far)]), DIL_PATTERNS[2][0]))
    for first in (True, False):
        tq = slab_tokens(B2_SLABS, B2_Q_ROWS, 0, 4)
        tk = slab_tokens(B2_SLABS, 2 * B2_Q_ROWS, 0 if first else -B2_Q_ROWS, 4)
        specs.append((tq, tk, DIL_PATTERNS[1][0]))
    for first in (True, False):
        tq = slab_tokens(B1_SLABS, B1_Q_ROWS, 0, 1)
        tk = slab_tokens(B1_SLABS, 2 * B1_Q_ROWS, 0 if first else -B1_Q_ROWS, 1)
        specs.append((tq, tk, DIL_PATTERNS[0][0]))
    buckets = []
    for tq, tk, window in specs:
        dist = tq[:, None] - tk[None, :]
        in_band = (dist >= 0) & (dist <= window)
        buckets.append(np.where(in_band, _t5_bucket(np.clip(dist, 0, None)), MASKED_BUCKET))
    return np.stack(buckets).astype(np.int32)


def _bias_kernel(rb_ref, bk_ref, o_ref):
    bk = bk_ref[...]
    for h in range(DIL_HEADS):
        acc = jnp.full(bk.shape, -jnp.inf, F32)
        for bucket in range(REL_BUCKETS):
            acc = jnp.where(bk == bucket, rb_ref[bucket, h] * LOG2E, acc)
        o_ref[h] = acc


def _bias_tables(rel_bias, buckets):
    nt, q, q2 = buckets.shape
    return pl.pallas_call(
        _bias_kernel,
        grid=(nt,),
        in_specs=[pl.BlockSpec(memory_space=pltpu.SMEM),
                  pl.BlockSpec((None, q, q2), lambda i: (i, 0, 0))],
        out_specs=pl.BlockSpec((DIL_HEADS, None, q, q2), lambda i: (0, i, 0, 0)),
        out_shape=jax.ShapeDtypeStruct((DIL_HEADS, nt, q, q2), F32),
        compiler_params=_cparams(("parallel",)),
        name="t5_bias_tables",
    )(rel_bias, buckets)


def _gather(ref, slabs):
    return jnp.concatenate([ref[pl.ds(s, n), :] for s, n in slabs], axis=0)


def _scatter(ref, slabs, val):
    off = 0
    for s, n in slabs:
        ref[pl.ds(s, n), :] = val[off:off + n]
        off += n


def _attn_kernel(q_ref, k_ref, v_ref, bias_ref, o_ref,
                 q32_s, k32_s, v32_s, acc_s, m_s, l_s):
    qb = DIL_BLOCK

    def run_group(blocks, init):
        olds = [None if init else (_gather(m_s, sl), _gather(l_s, sl), _gather(acc_s, sl))
                for *_, sl in blocks]
        logits = [lax.dot_general(q, kw, (((1,), (1,)), ((), ())), preferred_element_type=F32)
                  for q, kw, *_ in blocks]
        probs, stats = [], []
        for s, (_, _, _, tab, nk, _), old in zip(logits, blocks, olds):
            s = s + bias_ref[tab, :, 0:nk]
            m_blk = jnp.broadcast_to(jnp.max(s, axis=-1, keepdims=True), (qb, LANE))
            if init:
                m_new, alpha = m_blk, None
            else:
                m_new = jnp.maximum(old[0], m_blk)
                alpha = jnp.exp2(old[0] - m_new)
            p = jnp.exp2(s - jnp.concatenate([m_new] * (nk // LANE), axis=1))
            l_blk = jnp.broadcast_to(jnp.sum(p, axis=-1, keepdims=True), (qb, LANE))
            probs.append(p.astype(BF16))
            stats.append((m_new, alpha, l_blk))
        pvs = [jnp.dot(p, vw, preferred_element_type=F32) for p, (_, _, vw, *_) in zip(probs, blocks)]
        for pv, (m_new, alpha, l_blk), old, (*_, sl) in zip(pvs, stats, olds, blocks):
            _scatter(m_s, sl, m_new)
            _scatter(l_s, sl, l_blk if init else alpha * old[1] + l_blk)
            _scatter(acc_s, sl, pv if init else alpha * old[2] + pv)

    def b3_body(i, carry):
        blocks = []
        for jj in range(ATT_GROUP):
            r0 = pl.multiple_of((i * ATT_GROUP + jj) * CLASS_LEN, CLASS_LEN)
            rows = pl.ds(r0, CLASS_LEN)
            blocks.append((q_ref[rows, :], k_ref[rows, :], v_ref[rows, :], ATT_B3, CLASS_LEN,
                           [(r0, CLASS_LEN)]))
        run_group(blocks, init=True)
        return carry

    lax.fori_loop(0, N_CLASS // ATT_GROUP, b3_body, 0)

    n_res = DIL_PATTERNS[1][1]
    assert ATT_GROUP % n_res == 0

    def b2_body(i, carry):
        blocks = []
        for jj in range(ATT_GROUP):
            e = jj % n_res
            kk = i * (ATT_GROUP // n_res) + jj // n_res
            q0 = pl.multiple_of(kk * B2_Q_ROWS, B2_Q_ROWS)
            k0 = pl.multiple_of(jnp.maximum(q0 - B2_Q_ROWS, 0), B2_Q_ROWS)
            tab = jnp.where(kk == 0, ATT_B2_FIRST, ATT_B2_REST)
            bases = [(n_res * c + e) * CLASS_LEN for c in range(B2_SLABS)]
            q_slabs = [(pl.multiple_of(base + q0, B2_Q_ROWS), B2_Q_ROWS) for base in bases]
            k_slabs = [(pl.multiple_of(base + k0, B2_Q_ROWS), 2 * B2_Q_ROWS) for base in bases]
            blocks.append((_gather(q_ref, q_slabs), _gather(k_ref, k_slabs), _gather(v_ref, k_slabs),
                           tab, 2 * qb, q_slabs))
        run_group(blocks, init=False)
        return carry

    lax.fori_loop(0, n_res * (CLASS_LEN // B2_Q_ROWS) // ATT_GROUP, b2_body, 0)

    q32_s[...] = q_ref[...].astype(F32)
    k32_s[...] = k_ref[...].astype(F32)
    v32_s[...] = v_ref[...].astype(F32)

    def b1_body(i, carry):
        blocks = []
        for jj in range(ATT_GROUP):
            kk = i * ATT_GROUP + jj
            q0 = pl.multiple_of(kk * B1_Q_ROWS, B1_Q_ROWS)
            k0 = pl.multiple_of(jnp.maximum(q0 - B1_Q_ROWS, 0), B1_Q_ROWS)
            tab = jnp.where(kk == 0, ATT_B1_FIRST, ATT_B1_REST)
            q_slabs = [(pl.multiple_of(u * CLASS_LEN + q0, B1_Q_ROWS), B1_Q_ROWS)
                       for u in range(B1_SLABS)]
            k_slabs = [(pl.multiple_of(u * CLASS_LEN + k0, B1_Q_ROWS), 2 * B1_Q_ROWS)
                       for u in range(B1_SLABS)]
            blocks.append((_gather(q32_s, q_slabs).astype(BF16), _gather(k32_s, k_slabs).astype(BF16),
                           _gather(v32_s, k_slabs).astype(BF16), tab, 2 * qb, q_slabs))
        run_group(blocks, init=False)
        return carry

    lax.fori_loop(0, CLASS_LEN // B1_Q_ROWS // ATT_GROUP, b1_body, 0)

    o_ref[...] = (acc_s[...] / l_s[...]).astype(o_ref.dtype)


def _attention(qkv_p, bias_tabs):
    s = SEQ
    nt = bias_tabs.shape[1]
    heads = DIL_HEADS
    return pl.pallas_call(
        _attn_kernel,
        grid=(BATCH, heads),
        in_specs=[
            pl.BlockSpec((s, DIL_DH), lambda b, h: (b, h)),
            pl.BlockSpec((s, DIL_DH), lambda b, h: (b, heads + h)),
            pl.BlockSpec((s, DIL_DH), lambda b, h: (b, 2 * heads + h)),
            pl.BlockSpec((None, nt, DIL_BLOCK, 2 * DIL_BLOCK), lambda b, h: (h, 0, 0, 0)),
        ],
        out_specs=pl.BlockSpec((s, DIL_DH), lambda b, h: (b, h)),
        out_shape=jax.ShapeDtypeStruct((M_TOK, DIL_W), BF16),
        scratch_shapes=[
            pltpu.VMEM((s, DIL_DH), F32),
            pltpu.VMEM((s, DIL_DH), F32),
            pltpu.VMEM((s, DIL_DH), F32),
            pltpu.VMEM((s, DIL_DH), F32),
            pltpu.VMEM((s, LANE), F32),
            pltpu.VMEM((s, LANE), F32),
        ],
        compiler_params=_cparams(("parallel", "parallel")),
        name="dilated_attention",
    )(qkv_p, qkv_p, qkv_p, bias_tabs)


def _to_class_order(a):
    w = a.shape[1]
    return a.reshape(BATCH, CLASS_LEN, N_CLASS, w).transpose(0, 2, 1, 3).reshape(M_TOK, w)


def _to_token_order(a):
    w = a.shape[1]
    return a.reshape(BATCH, N_CLASS, CLASS_LEN, w).transpose(0, 2, 1, 3).reshape(M_TOK, w)


def _outproj_kernel(oa_ref, ob_ref, x_ref, w_ref, g_ref, x1_ref, h2_ref):
    acc = jnp.dot(oa_ref[...], w_ref[0:VA_W, :], preferred_element_type=F32)
    acc = acc + jnp.dot(ob_ref[...], w_ref[VA_W:, :], preferred_element_type=F32)
    x1 = x_ref[...] + acc
    x1_ref[...] = x1
    ms = jnp.mean(x1 * x1, axis=-1, keepdims=True)
    h2_ref[...] = (x1 * lax.rsqrt(ms + RMS_EPS) * g_ref[...]).astype(h2_ref.dtype)


def _outproj(out_a, out_b, x, w_out, layer, norm2_g):
    m, d = x.shape
    tm = TM_OUT
    row = lambda w: pl.BlockSpec((tm, w), lambda i: (i, 0))
    return pl.pallas_call(
        _outproj_kernel,
        grid=(m // tm,),
        in_specs=[row(VA_W), row(DIL_W), row(d),
                  pl.BlockSpec((None, d, d), lambda i: (layer, 0, 0)),
                  pl.BlockSpec((1, d), lambda i: (0, 0))],
        out_specs=[row(d), row(d)],
        out_shape=[jax.ShapeDtypeStruct((m, d), F32), jax.ShapeDtypeStruct((m, d), BF16)],
        compiler_params=_cparams(("parallel",)),
        name="outproj_norm2",
    )(out_a, out_b, x, w_out, norm2_g.reshape(1, d))


def _ffn_kernel(h_ref, x_ref, wg_ref, wu_ref, wd_ref, o_ref):
    j = pl.program_id(1)

    @pl.when(j == 0)
    def _():
        o_ref[...] = x_ref[...]

    h = h_ref[...]
    gate = jnp.dot(h, wg_ref[...].astype(BF16), preferred_element_type=F32)
    up = jnp.dot(h, wu_ref[...].astype(BF16), preferred_element_type=F32)
    act = (gate * jax.nn.sigmoid(gate) * up).astype(BF16)
    o_ref[...] += jnp.dot(act, wd_ref[...].astype(BF16), preferred_element_type=F32)


def _ffn(h2, x1, w_gate, w_up, w_down, layer):
    m, d = x1.shape
    f = w_gate.shape[2]
    tm, tf = TM_FFN, TF_FFN
    return pl.pallas_call(
        _ffn_kernel,
        grid=(m // tm, f // tf),
        in_specs=[pl.BlockSpec((tm, d), lambda i, j: (i, 0)),
                  pl.BlockSpec((tm, d), lambda i, j: (i, 0), pipeline_mode=pl.Buffered(1)),
                  pl.BlockSpec((None, d, tf), lambda i, j: (layer, 0, j)),
                  pl.BlockSpec((None, d, tf), lambda i, j: (layer, 0, j)),
                  pl.BlockSpec((None, tf, d), lambda i, j: (layer, j, 0))],
        out_specs=pl.BlockSpec((tm, d), lambda i, j: (i, 0)),
        out_shape=jax.ShapeDtypeStruct((m, d), F32),
        compiler_params=_cparams(("parallel", "arbitrary")),
        name="swiglu_ffn",
    )(h2, x1, w_gate, w_up, w_down)


def kernel(x, norm1_g, w_in, gla_gate_w2, gla_gate_b, gla_onorm_g, q_norm_g, k_norm_g, rel_bias,
           w_out, norm2_g, w_gate, w_up, w_down):
    bsz, s, d = x.shape
    assert (bsz, s, d) == (BATCH, SEQ, D_MODEL)
    bias_tabs = _bias_tables(rel_bias.astype(F32), jnp.asarray(_attention_tables()))
    w_out_b = w_out.astype(BF16)

    xf = x.reshape(M_TOK, D_MODEL)
    for l in range(DEPTH):
        w_plain = w_in[l][:, :PA_W].astype(BF16)
        w_gate1 = jnp.pad(w_in[l][:, GA_OFF:PB_OFF], ((0, 0), (0, GATE_PAD - GLA_GATE_RANK))).astype(BF16)
        w_gate2 = jnp.pad(gla_gate_w2[l], ((0, GATE_PAD - GLA_GATE_RANK), (0, 0))).astype(BF16)
        w_qkv = w_in[l][:, PB_OFF:].astype(BF16)
        qk_gain = jnp.concatenate([jnp.tile(q_norm_g[l], DIL_HEADS) * (DIL_DH ** -0.5 * LOG2E),
                                   jnp.tile(k_norm_g[l], DIL_HEADS)]).reshape(1, 2 * DIL_W).astype(F32)

        h = _rmsnorm(xf, norm1_g[l])
        pa = _proj_plain(h, w_plain)
        log_g = _gate(h, w_gate1, w_gate2, gla_gate_b[l].reshape(1, QA_W).astype(F32))
        out_a = _gla(pa, log_g, gla_onorm_g[l])

        qkv_p = _proj_qkv(_to_class_order(h), w_qkv, qk_gain)
        out_b = _to_token_order(_attention(qkv_p, bias_tabs))

        x1, h2 = _outproj(out_a, out_b, xf, w_out_b, l, norm2_g[l])
        xf = _ffn(h2, x1, w_gate, w_up, w_down, l)
    return xf.reshape(bsz, s, d)
```

```python
import math

import numpy as np
import jax
import jax.numpy as jnp
from jax import lax
from jax.experimental import pallas as pl
from jax.experimental.pallas import tpu as pltpu

D_MODEL = 2048
BATCH = 4
SEQ = 2048
DEPTH = 2
M_TOK = BATCH * SEQ

GLA_HEADS = 4
GLA_DV = 256
GLA_DK = 128
GLA_GATE_RANK = 16
GLA_GATE_TAU = 16.0
GLA_CHUNK = 64
DIL_HEADS = 8
DIL_DH = 128
DIL_PATTERNS = ((128, 1), (512, 4), (2048, 16))
DIL_BLOCK = 128
REL_BUCKETS = 32
REL_MAX_DIST = 2048
FFN_HIDDEN = 5632
RMS_EPS = 1e-6

QA_W = GLA_HEADS * GLA_DK
VA_W = GLA_HEADS * GLA_DV
DIL_W = DIL_HEADS * DIL_DH
PA_QA, PA_KA, PA_VA, PA_RA = 0, QA_W, 2 * QA_W, 2 * QA_W + VA_W
PA_W = PA_RA + VA_W
GA_OFF = PA_W
PB_OFF = GA_OFF + GLA_GATE_RANK
PB_W = 3 * DIL_W
LANE = 128
MXU_N = 256
F32_ROWS = 8
GATE_PAD = LANE

N_CLASS = 16
CLASS_LEN = SEQ // N_CLASS

BF16 = jnp.bfloat16
F32 = jnp.float32

VMEM_LIMIT = 56 * 1024 * 1024

TM_NORM = 512
TM_PROJ, TN_PROJ = 1024, 1024
TM_GATE = 1024
TM_OUT = 256
TM_FFN, TF_FFN = 1024, 256
GLA_GROUP = 16
ATT_GROUP = 8
LOG2E = math.log2(math.e)


def _cparams(sem):
    return pltpu.CompilerParams(dimension_semantics=sem, vmem_limit_bytes=VMEM_LIMIT)


def _rmsnorm_kernel(x_ref, g_ref, o_ref):
    x = x_ref[...]
    ms = jnp.mean(x * x, axis=-1, keepdims=True)
    o_ref[...] = (x * lax.rsqrt(ms + RMS_EPS) * g_ref[...]).astype(o_ref.dtype)


def _rmsnorm(x, g):
    m, d = x.shape
    return pl.pallas_call(
        _rmsnorm_kernel,
        grid=(m // TM_NORM,),
        in_specs=[pl.BlockSpec((TM_NORM, d), lambda i: (i, 0)),
                  pl.BlockSpec((1, d), lambda i: (0, 0))],
        out_specs=pl.BlockSpec((TM_NORM, d), lambda i: (i, 0)),
        out_shape=jax.ShapeDtypeStruct((m, d), BF16),
        compiler_params=_cparams(("parallel",)),
        name="rmsnorm",
    )(x, g.reshape(1, d))


def _proj_plain_kernel(h_ref, w_ref, o_ref):
    o_ref[...] = jnp.dot(h_ref[...], w_ref[...], preferred_element_type=F32).astype(o_ref.dtype)


def _proj_plain(h, w):
    m, k = h.shape
    n = w.shape[1]
    return pl.pallas_call(
        _proj_plain_kernel,
        grid=(n // TN_PROJ, m // TM_PROJ),
        in_specs=[pl.BlockSpec((TM_PROJ, k), lambda j, i: (i, 0)),
                  pl.BlockSpec((k, TN_PROJ), lambda j, i: (0, j))],
        out_specs=pl.BlockSpec((TM_PROJ, TN_PROJ), lambda j, i: (i, j)),
        out_shape=jax.ShapeDtypeStruct((m, n), BF16),
        compiler_params=_cparams(("parallel", "parallel")),
        name="proj_plain",
    )(h, w)


def _proj_qk_kernel(h_ref, w_ref, g_ref, o_ref):
    h = h_ref[...]
    for n in range(o_ref.shape[1] // MXU_N):
        acc = jnp.dot(h, w_ref[:, n * MXU_N:(n + 1) * MXU_N], preferred_element_type=F32)
        for hh in range(MXU_N // DIL_DH):
            cs = slice(n * MXU_N + hh * DIL_DH, n * MXU_N + (hh + 1) * DIL_DH)
            a = acc[:, hh * DIL_DH:(hh + 1) * DIL_DH]
            ms = jnp.mean(a * a, axis=-1, keepdims=True)
            o_ref[:, cs] = (a * lax.rsqrt(ms + RMS_EPS) * g_ref[:, cs]).astype(o_ref.dtype)


def _proj_qk(h, w, g):
    m, k = h.shape
    n = w.shape[1]
    return pl.pallas_call(
        _proj_qk_kernel,
        grid=(n // TN_PROJ, m // TM_PROJ),
        in_specs=[pl.BlockSpec((TM_PROJ, k), lambda j, i: (i, 0)),
                  pl.BlockSpec((k, TN_PROJ), lambda j, i: (0, j)),
                  pl.BlockSpec((1, TN_PROJ), lambda j, i: (0, j))],
        out_specs=pl.BlockSpec((TM_PROJ, TN_PROJ), lambda j, i: (i, j)),
        out_shape=jax.ShapeDtypeStruct((m, n), BF16),
        compiler_params=_cparams(("parallel", "parallel")),
        name="proj_qknorm",
    )(h, w, g)


def _gate_kernel(h_ref, w1_ref, w2_ref, b_ref, o_ref):
    ga = jnp.dot(h_ref[...], w1_ref[...], preferred_element_type=F32)
    pre = jnp.dot(ga.astype(BF16), w2_ref[...], preferred_element_type=F32) + b_ref[...]
    o_ref[...] = jax.nn.log_sigmoid(pre) * (LOG2E / GLA_GATE_TAU)


def _gate(h, w1, w2, b):
    m, k = h.shape
    n = w2.shape[1]
    return pl.pallas_call(
        _gate_kernel,
        grid=(m // TM_GATE,),
        in_specs=[pl.BlockSpec((TM_GATE, k), lambda i: (i, 0)),
                  pl.BlockSpec((k, GATE_PAD), lambda i: (0, 0)),
                  pl.BlockSpec((GATE_PAD, n), lambda i: (0, 0)),
                  pl.BlockSpec((1, n), lambda i: (0, 0))],
        out_specs=pl.BlockSpec((TM_GATE, n), lambda i: (i, 0)),
        out_shape=jax.ShapeDtypeStruct((m, n), F32),
        compiler_params=_cparams(("parallel",)),
        name="gla_gate",
    )(h, w1, w2, b)


def _gla_kernel(q_ref, k_ref, v_ref, r_ref, lg_ref, gn_ref, o_ref,
                qt_s, kt_s, kd_s, dec_s, st_s, o_s):
    s_len, c, n_chunks = SEQ, GLA_CHUNK, SEQ // GLA_CHUNK
    b = lg_ref[...]
    row = lax.broadcasted_iota(jnp.int32, b.shape, 0) & (c - 1)
    shift = 1
    while shift < c:
        b = b + jnp.where(row >= shift, pltpu.roll(b, shift, axis=0), 0.0)
        shift *= 2
    def chunk_last(a):
        a3 = a.reshape(n_chunks, c, GLA_DK)
        return jnp.broadcast_to(a3[:, c - 1:c, :], a3.shape).reshape(s_len, GLA_DK)

    eb = jnp.exp2(b)
    q = q_ref[...].astype(F32) * (GLA_DK ** -0.5)
    k = k_ref[...].astype(F32)
    qt_s[...] = (q * eb).astype(BF16)
    kt_s[...] = (k * jnp.exp2(-b)).astype(BF16)
    kd_s[...] = (k * jnp.exp2(chunk_last(b) - b)).astype(BF16)
    dec_s[...] = chunk_last(eb)
    st_s[...] = jnp.zeros_like(st_s)

    ri = lax.broadcasted_iota(jnp.int32, (c, c), 0)
    ci = lax.broadcasted_iota(jnp.int32, (c, c), 1)
    causal = ri >= ci

    nt_dims = (((1,), (1,)), ((), ()))
    tn_dims = (((0,), (0,)), ((), ()))

    def chunk_group(gi, carry):
        starts = [pl.multiple_of((gi * GLA_GROUP + j) * c, c) for j in range(GLA_GROUP)]
        rows = [pl.ds(r0, c) for r0 in starts]
        qts = [qt_s[r, :] for r in rows]
        vs = [v_ref[r, :] for r in rows]
        attn = [lax.dot_general(qt, kt_s[r, :], nt_dims, preferred_element_type=F32)
                for qt, r in zip(qts, rows)]
        cs_t = [lax.dot_general(v, kd_s[r, :], tn_dims, preferred_element_type=F32)
                for v, r in zip(vs, rows)]
        attn = [jnp.where(causal, a, 0.0).astype(BF16) for a in attn]
        o_intra = [jnp.dot(a, v, preferred_element_type=F32) for a, v in zip(attn, vs)]
        st = st_s[...]
        states = []
        for r0, cs in zip(starts, cs_t):
            states.append(st.astype(BF16))
            st = st * dec_s[pl.ds(r0, F32_ROWS), :][0:1, :] + cs
        st_s[...] = st
        o_inter = [lax.dot_general(qt, sb, nt_dims, preferred_element_type=F32)
                   for qt, sb in zip(qts, states)]
        for r, a, b_ in zip(rows, o_intra, o_inter):
            o_s[r, :] = a + b_
        return carry

    lax.fori_loop(0, n_chunks // GLA_GROUP, chunk_group, 0)

    o = o_s[...]
    ms = jnp.mean(o * o, axis=-1, keepdims=True)
    y = o * lax.rsqrt(ms + RMS_EPS) * gn_ref[...]
    r = r_ref[...].astype(F32)
    o_ref[...] = (y * (r * jax.nn.sigmoid(r))).astype(o_ref.dtype)


def _gla(pa, log_g, onorm_g):
    s = SEQ
    return pl.pallas_call(
        _gla_kernel,
        grid=(BATCH, GLA_HEADS),
        in_specs=[
            pl.BlockSpec((s, GLA_DK), lambda b, h: (b, PA_QA // GLA_DK + h)),
            pl.BlockSpec((s, GLA_DK), lambda b, h: (b, PA_KA // GLA_DK + h)),
            pl.BlockSpec((s, GLA_DV), lambda b, h: (b, PA_VA // GLA_DV + h)),
            pl.BlockSpec((s, GLA_DV), lambda b, h: (b, PA_RA // GLA_DV + h)),
            pl.BlockSpec((s, GLA_DK), lambda b, h: (b, h)),
            pl.BlockSpec((1, GLA_DV), lambda b, h: (0, 0)),
        ],
        out_specs=pl.BlockSpec((s, GLA_DV), lambda b, h: (b, h)),
        out_shape=jax.ShapeDtypeStruct((M_TOK, VA_W), BF16),
        scratch_shapes=[
            pltpu.VMEM((s, GLA_DK), BF16),
            pltpu.VMEM((s, GLA_DK), BF16),
            pltpu.VMEM((s, GLA_DK), BF16),
            pltpu.VMEM((s, GLA_DK), F32),
            pltpu.VMEM((GLA_DV, GLA_DK), F32),
            pltpu.VMEM((s, GLA_DV), F32),
        ],
        compiler_params=_cparams(("parallel", "parallel")),
        name="gla_mixer",
    )(pa, pa, pa, pa, log_g, onorm_g.reshape(1, GLA_DV))


def _t5_bucket(dist):
    max_exact = REL_BUCKETS // 2
    safe = np.maximum(dist, 1)
    large = max_exact + (np.log(safe / max_exact) / np.log(REL_MAX_DIST / max_exact)
                         * (REL_BUCKETS - max_exact)).astype(np.int64)
    large = np.minimum(large, REL_BUCKETS - 1)
    return np.where(dist < max_exact, dist, large).astype(np.int32)


ATT_B3, ATT_B2_FIRST, ATT_B2_REST, ATT_B1_FIRST, ATT_B1_REST = range(5)
B2_SLABS, B2_Q_ROWS = 4, 32
B1_SLABS, B1_Q_ROWS = 16, 8
MASKED_BUCKET = -1


def _attention_tables():
    far = 10 ** 6

    def slab_tokens(n_slabs, rows, first_pos, class_step):
        pos = first_pos + np.arange(rows)
        return (N_CLASS * pos[None, :] + class_step * np.arange(n_slabs)[:, None]).reshape(-1)

    specs = []
    a = np.arange(CLASS_LEN)
    specs.append((N_CLASS * a, np.concatenate([N_CLASS * a, np.full(CLASS_LEN, far)]), DIL_PATTERNS[2][0]))
    for first in (True, False):
        tq = slab_tokens(B2_SLABS, B2_Q_ROWS, 0, 4)
        tk = slab_tokens(B2_SLABS, 2 * B2_Q_ROWS, 0 if first else -B2_Q_ROWS, 4)
        specs.append((tq, tk, DIL_PATTERNS[1][0]))
    for first in (True, False):
        tq = slab_tokens(B1_SLABS, B1_Q_ROWS, 0, 1)
        tk = slab_tokens(B1_SLABS, 2 * B1_Q_ROWS, 0 if first else -B1_Q_ROWS, 1)
        specs.append((tq, tk, DIL_PATTERNS[0][0]))
    buckets = []
    for tq, tk, window in specs:
        dist = tq[:, None] - tk[None, :]
        in_band = (dist >= 0) & (dist <= window)
        buckets.append(np.where(in_band, _t5_bucket(np.clip(dist, 0, None)), MASKED_BUCKET))
    return np.stack(buckets).astype(np.int32)


def _bias_kernel(rb_ref, bk_ref, o_ref):
    bk = bk_ref[...]
    for h in range(DIL_HEADS):
        acc = jnp.full(bk.shape, -jnp.inf, F32)
        for bucket in range(REL_BUCKETS):
            acc = jnp.where(bk == bucket, rb_ref[bucket, h] * LOG2E, acc)
        o_ref[h] = acc


def _bias_tables(rel_bias, buckets):
    nt, q, q2 = buckets.shape
    return pl.pallas_call(
        _bias_kernel,
        grid=(nt,),
        in_specs=[pl.BlockSpec(memory_space=pltpu.SMEM),
                  pl.BlockSpec((None, q, q2), lambda i: (i, 0, 0))],
        out_specs=pl.BlockSpec((DIL_HEADS, None, q, q2), lambda i: (0, i, 0, 0)),
        out_shape=jax.ShapeDtypeStruct((DIL_HEADS, nt, q, q2), F32),
        compiler_params=_cparams(("parallel",)),
        name="t5_bias_tables",
    )(rel_bias, buckets)


def _gather(ref, slabs):
    return jnp.concatenate([ref[pl.ds(s, n), :] for s, n in slabs], axis=0)


def _scatter(ref, slabs, val):
    off = 0
    for s, n in slabs:
        ref[pl.ds(s, n), :] = val[off:off + n]
        off += n


def _attn_kernel(q_ref, k_ref, v_ref, bias_ref, o_ref,
                 q32_s, k32_s, v32_s, acc_s, m_s, l_s):
    qb = DIL_BLOCK

    def run_group(blocks, init):
        olds = [None if init else (_gather(m_s, sl), _gather(l_s, sl), _gather(acc_s, sl))
                for *_, sl in blocks]
        logits = [lax.dot_general(q, kw, (((1,), (1,)), ((), ())), preferred_element_type=F32)
                  for q, kw, *_ in blocks]
        probs, stats = [], []
        for s, (_, _, _, tab, nk, _), old in zip(logits, blocks, olds):
            s = s + bias_ref[tab, :, 0:nk]
            m_blk = jnp.broadcast_to(jnp.max(s, axis=-1, keepdims=True), (qb, LANE))
            if init:
                m_new, alpha = m_blk, None
            else:
                m_new = jnp.maximum(old[0], m_blk)
                alpha = jnp.exp2(old[0] - m_new)
            p = jnp.exp2(s - jnp.concatenate([m_new] * (nk // LANE), axis=1))
            l_blk = jnp.broadcast_to(jnp.sum(p, axis=-1, keepdims=True), (qb, LANE))
            probs.append(p.astype(BF16))
            stats.append((m_new, alpha, l_blk))
        pvs = [jnp.dot(p, vw, preferred_element_type=F32) for p, (_, _, vw, *_) in zip(probs, blocks)]
        for pv, (m_new, alpha, l_blk), old, (*_, sl) in zip(pvs, stats, olds, blocks):
            _scatter(m_s, sl, m_new)
            _scatter(l_s, sl, l_blk if init else alpha * old[1] + l_blk)
            _scatter(acc_s, sl, pv if init else alpha * old[2] + pv)

    def b3_body(i, carry):
        blocks = []
        for jj in range(ATT_GROUP):
            r0 = pl.multiple_of((i * ATT_GROUP + jj) * CLASS_LEN, CLASS_LEN)
            rows = pl.ds(r0, CLASS_LEN)
            blocks.append((q_ref[rows, :], k_ref[rows, :], v_ref[rows, :], ATT_B3, CLASS_LEN,
                           [(r0, CLASS_LEN)]))
        run_group(blocks, init=True)
        return carry

    lax.fori_loop(0, N_CLASS // ATT_GROUP, b3_body, 0)

    n_res = DIL_PATTERNS[1][1]
    assert ATT_GROUP % n_res == 0

    def b2_body(i, carry):
        blocks = []
        for jj in range(ATT_GROUP):
            e = jj % n_res
            kk = i * (ATT_GROUP // n_res) + jj // n_res
            q0 = pl.multiple_of(kk * B2_Q_ROWS, B2_Q_ROWS)
            k0 = pl.multiple_of(jnp.maximum(q0 - B2_Q_ROWS, 0), B2_Q_ROWS)
            tab = jnp.where(kk == 0, ATT_B2_FIRST, ATT_B2_REST)
            bases = [(n_res * c + e) * CLASS_LEN for c in range(B2_SLABS)]
            q_slabs = [(pl.multiple_of(base + q0, B2_Q_ROWS), B2_Q_ROWS) for base in bases]
            k_slabs = [(pl.multiple_of(base + k0, B2_Q_ROWS), 2 * B2_Q_ROWS) for base in bases]
            blocks.append((_gather(q_ref, q_slabs), _gather(k_ref, k_slabs), _gather(v_ref, k_slabs),
                           tab, 2 * qb, q_slabs))
        run_group(blocks, init=False)
        return carry

    lax.fori_loop(0, n_res * (CLASS_LEN // B2_Q_ROWS) // ATT_GROUP, b2_body, 0)

    q32_s[...] = q_ref[...].astype(F32)
    k32_s[...] = k_ref[...].astype(F32)
    v32_s[...] = v_ref[...].astype(F32)

    def b1_body(i, carry):
        blocks = []
        for jj in range(ATT_GROUP):
            kk = i * ATT_GROUP + jj
            q0 = pl.multiple_of(kk * B1_Q_ROWS, B1_Q_ROWS)
            k0 = pl.multiple_of(jnp.maximum(q0 - B1_Q_ROWS, 0), B1_Q_ROWS)
            tab = jnp.where(kk == 0, ATT_B1_FIRST, ATT_B1_REST)
            q_slabs = [(pl.multiple_of(u * CLASS_LEN + q0, B1_Q_ROWS), B1_Q_ROWS)
                       for u in range(B1_SLABS)]
            k_slabs = [(pl.multiple_of(u * CLASS_LEN + k0, B1_Q_ROWS), 2 * B1_Q_ROWS)
                       for u in range(B1_SLABS)]
            blocks.append((_gather(q32_s, q_slabs).astype(BF16), _gather(k32_s, k_slabs).astype(BF16),
                           _gather(v32_s, k_slabs).astype(BF16), tab, 2 * qb, q_slabs))
        run_group(blocks, init=False)
        return carry

    lax.fori_loop(0, CLASS_LEN // B1_Q_ROWS // ATT_GROUP, b1_body, 0)

    o_ref[...] = (acc_s[...] / l_s[...]).astype(o_ref.dtype)


def _attention(qk_p, v_p, bias_tabs):
    s = SEQ
    nt = bias_tabs.shape[1]
    heads = DIL_HEADS
    return pl.pallas_call(
        _attn_kernel,
        grid=(BATCH, heads),
        in_specs=[
            pl.BlockSpec((s, DIL_DH), lambda b, h: (b, h)),
            pl.BlockSpec((s, DIL_DH), lambda b, h: (b, heads + h)),
            pl.BlockSpec((s, DIL_DH), lambda b, h: (b, h)),
            pl.BlockSpec((None, nt, DIL_BLOCK, 2 * DIL_BLOCK), lambda b, h: (h, 0, 0, 0)),
        ],
        out_specs=pl.BlockSpec((s, DIL_DH), lambda b, h: (b, h)),
        out_shape=jax.ShapeDtypeStruct((M_TOK, DIL_W), BF16),
        scratch_shapes=[
            pltpu.VMEM((s, DIL_DH), F32),
            pltpu.VMEM((s, DIL_DH), F32),
            pltpu.VMEM((s, DIL_DH), F32),
            pltpu.VMEM((s, DIL_DH), F32),
            pltpu.VMEM((s, LANE), F32),
            pltpu.VMEM((s, LANE), F32),
        ],
        compiler_params=_cparams(("parallel", "parallel")),
        name="dilated_attention",
    )(qk_p, qk_p, v_p, bias_tabs)


def _to_class_order(a):
    w = a.shape[1]
    return a.reshape(BATCH, CLASS_LEN, N_CLASS, w).transpose(0, 2, 1, 3).reshape(M_TOK, w)


def _to_token_order(a):
    w = a.shape[1]
    return a.reshape(BATCH, N_CLASS, CLASS_LEN, w).transpose(0, 2, 1, 3).reshape(M_TOK, w)


def _outproj_kernel(oa_ref, ob_ref, x_ref, w_ref, g_ref, x1_ref, h2_ref):
    acc = jnp.dot(oa_ref[...], w_ref[0:VA_W, :], preferred_element_type=F32)
    acc = acc + jnp.dot(ob_ref[...], w_ref[VA_W:, :], preferred_element_type=F32)
    x1 = x_ref[...] + acc
    x1_ref[...] = x1
    ms = jnp.mean(x1 * x1, axis=-1, keepdims=True)
    h2_ref[...] = (x1 * lax.rsqrt(ms + RMS_EPS) * g_ref[...]).astype(h2_ref.dtype)


def _outproj(out_a, out_b, x, w_out, layer, norm2_g):
    m, d = x.shape
    tm = TM_OUT
    row = lambda w: pl.BlockSpec((tm, w), lambda i: (i, 0))
    return pl.pallas_call(
        _outproj_kernel,
        grid=(m // tm,),
        in_specs=[row(VA_W), row(DIL_W), row(d),
                  pl.BlockSpec((None, d, d), lambda i: (layer, 0, 0)),
                  pl.BlockSpec((1, d), lambda i: (0, 0))],
        out_specs=[row(d), row(d)],
        out_shape=[jax.ShapeDtypeStruct((m, d), F32), jax.ShapeDtypeStruct((m, d), BF16)],
        compiler_params=_cparams(("parallel",)),
        name="outproj_norm2",
    )(out_a, out_b, x, w_out, norm2_g.reshape(1, d))


def _ffn_kernel(h_ref, x_ref, wg_ref, wu_ref, wd_ref, o_ref):
    j = pl.program_id(1)

    @pl.when(j == 0)
    def _():
        o_ref[...] = x_ref[...]

    h = h_ref[...]
    gate = jnp.dot(h, wg_ref[...].astype(BF16), preferred_element_type=F32)
    up = jnp.dot(h, wu_ref[...].astype(BF16), preferred_element_type=F32)
    act = (gate * jax.nn.sigmoid(gate) * up).astype(BF16)
    o_ref[...] += jnp.dot(act, wd_ref[...].astype(BF16), preferred_element_type=F32)


def _ffn(h2, x1, w_gate, w_up, w_down, layer):
    m, d = x1.shape
    f = w_gate.shape[2]
    tm, tf = TM_FFN, TF_FFN
    return pl.pallas_call(
        _ffn_kernel,
        grid=(m // tm, f // tf),
        in_specs=[pl.BlockSpec((tm, d), lambda i, j: (i, 0)),
                  pl.BlockSpec((tm, d), lambda i, j: (i, 0)),
                  pl.BlockSpec((None, d, tf), lambda i, j: (layer, 0, j)),
                  pl.BlockSpec((None, d, tf), lambda i, j: (layer, 0, j)),
                  pl.BlockSpec((None, tf, d), lambda i, j: (layer, j, 0))],
        out_specs=pl.BlockSpec((tm, d), lambda i, j: (i, 0)),
        out_shape=jax.ShapeDtypeStruct((m, d), F32),
        compiler_params=_cparams(("parallel", "arbitrary")),
        name="swiglu_ffn",
    )(h2, x1, w_gate, w_up, w_down)


def kernel(x, norm1_g, w_in, gla_gate_w2, gla_gate_b, gla_onorm_g, q_norm_g, k_norm_g, rel_bias,
           w_out, norm2_g, w_gate, w_up, w_down):
    bsz, s, d = x.shape
    assert (bsz, s, d) == (BATCH, SEQ, D_MODEL)
    bias_tabs = _bias_tables(rel_bias.astype(F32), jnp.asarray(_attention_tables()))
    w_out_b = w_out.astype(BF16)

    xf = x.reshape(M_TOK, D_MODEL)
    for l in range(DEPTH):
        w_plain = w_in[l][:, :PA_W].astype(BF16)
        w_gate1 = jnp.pad(w_in[l][:, GA_OFF:PB_OFF], ((0, 0), (0, GATE_PAD - GLA_GATE_RANK))).astype(BF16)
        w_gate2 = jnp.pad(gla_gate_w2[l], ((0, GATE_PAD - GLA_GATE_RANK), (0, 0))).astype(BF16)
        w_qk = w_in[l][:, PB_OFF:PB_OFF + 2 * DIL_W].astype(BF16)
        w_v = w_in[l][:, PB_OFF + 2 * DIL_W:].astype(BF16)
        qk_gain = jnp.concatenate([jnp.tile(q_norm_g[l], DIL_HEADS) * (DIL_DH ** -0.5 * LOG2E),
                                   jnp.tile(k_norm_g[l], DIL_HEADS)]).reshape(1, 2 * DIL_W).astype(F32)

        h = _rmsnorm(xf, norm1_g[l])
        pa = _proj_plain(h, w_plain)
        log_g = _gate(h, w_gate1, w_gate2, gla_gate_b[l].reshape(1, QA_W).astype(F32))
        out_a = _gla(pa, log_g, gla_onorm_g[l])

        h_p = _to_class_order(h)
        out_b = _to_token_order(_attention(_proj_qk(h_p, w_qk, qk_gain), _proj_plain(h_p, w_v), bias_tabs))

        x1, h2 = _outproj(out_a, out_b, xf, w_out_b, l, norm2_g[l])
        xf = _ffn(h2, x1, w_gate, w_up, w_down, l)
    return xf.reshape(bsz, s, d)
```

```python
import functools
import math

import numpy as np
import jax
import jax.numpy as jnp
from jax import lax
from jax.experimental import pallas as pl
from jax.experimental.pallas import tpu as pltpu

D_MODEL = 2048
BATCH = 4
SEQ = 2048
DEPTH = 2
M_TOK = BATCH * SEQ

GLA_HEADS = 4
GLA_DV = 256
GLA_DK = 128
GLA_GATE_RANK = 16
GLA_GATE_TAU = 16.0
GLA_CHUNK = 64
DIL_HEADS = 8
DIL_DH = 128
DIL_PATTERNS = ((128, 1), (512, 4), (2048, 16))
DIL_BLOCK = 128
REL_BUCKETS = 32
REL_MAX_DIST = 2048
FFN_HIDDEN = 5632
RMS_EPS = 1e-6

QA_W = GLA_HEADS * GLA_DK
VA_W = GLA_HEADS * GLA_DV
DIL_W = DIL_HEADS * DIL_DH
PA_QA, PA_KA, PA_VA, PA_RA = 0, QA_W, 2 * QA_W, 2 * QA_W + VA_W
PA_W = PA_RA + VA_W
GA_OFF = PA_W
PB_OFF = GA_OFF + GLA_GATE_RANK
PB_W = 3 * DIL_W
LANE = 128
MXU_N = 256
F32_ROWS = 8
GATE_PAD = LANE

N_CLASS = 16
CLASS_LEN = SEQ // N_CLASS

BF16 = jnp.bfloat16
F32 = jnp.float32

VMEM_LIMIT = 56 * 1024 * 1024

TM_NORM = 512
TM_PROJ, TN_PROJ = 1024, 1024
TM_GATE = 1024
TM_OUT, OUT_CHUNK = 512, 512
TM_FFN, TF_FFN = 1024, 256
GLA_GROUP = 16
ATT_GROUP = 16
LOG2E = math.log2(math.e)


def _cparams(sem):
    return pltpu.CompilerParams(dimension_semantics=sem, vmem_limit_bytes=VMEM_LIMIT)


def _rmsnorm_kernel(x_ref, g_ref, o_ref):
    x = x_ref[...]
    ms = jnp.mean(x * x, axis=-1, keepdims=True)
    o_ref[...] = (x * lax.rsqrt(ms + RMS_EPS) * g_ref[...]).astype(o_ref.dtype)


def _rmsnorm(x, g):
    m, d = x.shape
    return pl.pallas_call(
        _rmsnorm_kernel,
        grid=(m // TM_NORM,),
        in_specs=[pl.BlockSpec((TM_NORM, d), lambda i: (i, 0)),
                  pl.BlockSpec((1, d), lambda i: (0, 0))],
        out_specs=pl.BlockSpec((TM_NORM, d), lambda i: (i, 0)),
        out_shape=jax.ShapeDtypeStruct((m, d), BF16),
        compiler_params=_cparams(("parallel",)),
        name="rmsnorm",
    )(x, g.reshape(1, d))


NT_DIMS = (((1,), (1,)), ((), ()))


def _proj_wt_kernel(h_ref, w_ref, *rest, qk_norm):
    g_ref = rest[0] if qk_norm else None
    o_ref, wb_s = rest[-2:]

    @pl.when(pl.program_id(1) == 0)
    def _():
        wb_s[...] = w_ref[0].astype(BF16)

    h = h_ref[...]
    if not qk_norm:
        o_ref[...] = lax.dot_general(h, wb_s[...], NT_DIMS, preferred_element_type=F32).astype(o_ref.dtype)
        return
    for n in range(o_ref.shape[1] // MXU_N):
        acc = lax.dot_general(h, wb_s[n * MXU_N:(n + 1) * MXU_N, :], NT_DIMS, preferred_element_type=F32)
        for hh in range(MXU_N // DIL_DH):
            cs = slice(n * MXU_N + hh * DIL_DH, n * MXU_N + (hh + 1) * DIL_DH)
            a = acc[:, hh * DIL_DH:(hh + 1) * DIL_DH]
            ms = jnp.mean(a * a, axis=-1, keepdims=True)
            o_ref[:, cs] = (a * lax.rsqrt(ms + RMS_EPS) * g_ref[:, cs]).astype(o_ref.dtype)


def _proj_wt(h, w_in_t, layer, col0, n, gain=None, name="proj"):
    m, k = h.shape
    qk_norm = gain is not None
    in_specs = [pl.BlockSpec((TM_PROJ, k), lambda j, i: (i, 0)),
                pl.BlockSpec((pl.Element(1), pl.Element(TN_PROJ), pl.Element(k)),
                             lambda j, i: (layer, pl.multiple_of(col0 + j * TN_PROJ, F32_ROWS), 0))]
    operands = [h, w_in_t]
    if qk_norm:
        in_specs.append(pl.BlockSpec((1, TN_PROJ), lambda j, i: (0, j)))
        operands.append(gain)
    return pl.pallas_call(
        functools.partial(_proj_wt_kernel, qk_norm=qk_norm),
        grid=(n // TN_PROJ, m // TM_PROJ),
        in_specs=in_specs,
        out_specs=pl.BlockSpec((TM_PROJ, TN_PROJ), lambda j, i: (i, j)),
        out_shape=jax.ShapeDtypeStruct((m, n), BF16),
        scratch_shapes=[pltpu.VMEM((TN_PROJ, k), BF16)],
        compiler_params=_cparams(("parallel", "arbitrary")),
        name=name,
    )(*operands)


def _gate_kernel(h_ref, w1_ref, w2_ref, b_ref, o_ref):
    w1 = w1_ref[...].astype(BF16)
    w1 = jnp.concatenate([w1, jnp.zeros((GATE_PAD - w1.shape[0], w1.shape[1]), BF16)], axis=0)
    ga = lax.dot_general(h_ref[...], w1, NT_DIMS, preferred_element_type=F32)
    pre = jnp.dot(ga.astype(BF16), w2_ref[...], preferred_element_type=F32) + b_ref[...]
    o_ref[...] = jax.nn.log_sigmoid(pre) * (LOG2E / GLA_GATE_TAU)


def _gate(h, w_in_t, layer, w2, b):
    m, k = h.shape
    n = w2.shape[1]
    assert GA_OFF % GLA_GATE_RANK == 0
    return pl.pallas_call(
        _gate_kernel,
        grid=(m // TM_GATE,),
        in_specs=[pl.BlockSpec((TM_GATE, k), lambda i: (i, 0)),
                  pl.BlockSpec((None, GLA_GATE_RANK, k), lambda i: (layer, GA_OFF // GLA_GATE_RANK, 0)),
                  pl.BlockSpec((GATE_PAD, n), lambda i: (0, 0)),
                  pl.BlockSpec((1, n), lambda i: (0, 0))],
        out_specs=pl.BlockSpec((TM_GATE, n), lambda i: (i, 0)),
        out_shape=jax.ShapeDtypeStruct((m, n), F32),
        compiler_params=_cparams(("parallel",)),
        name="gla_gate",
    )(h, w_in_t, w2, b)


def _gla_kernel(q_ref, k_ref, v_ref, r_ref, lg_ref, gn_ref, o_ref,
                qt_s, kt_s, kd_s, dec_s, st_s, o_s):
    s_len, c, n_chunks = SEQ, GLA_CHUNK, SEQ // GLA_CHUNK
    b = lg_ref[...]
    row = lax.broadcasted_iota(jnp.int32, b.shape, 0) & (c - 1)
    shift = 1
    while shift < c:
        b = b + jnp.where(row >= shift, pltpu.roll(b, shift, axis=0), 0.0)
        shift *= 2
    def chunk_last(a):
        a3 = a.reshape(n_chunks, c, GLA_DK)
        return jnp.broadcast_to(a3[:, c - 1:c, :], a3.shape).reshape(s_len, GLA_DK)

    eb = jnp.exp2(b)
    q = q_ref[...].astype(F32) * (GLA_DK ** -0.5)
    k = k_ref[...].astype(F32)
    qt_s[...] = (q * eb).astype(BF16)
    kt_s[...] = (k * jnp.exp2(-b)).astype(BF16)
    kd_s[...] = (k * jnp.exp2(chunk_last(b) - b)).astype(BF16)
    dec_s[...] = chunk_last(eb)
    st_s[...] = jnp.zeros_like(st_s)

    ri = lax.broadcasted_iota(jnp.int32, (c, c), 0)
    ci = lax.broadcasted_iota(jnp.int32, (c, c), 1)
    causal = ri >= ci

    nt_dims = (((1,), (1,)), ((), ()))
    tn_dims = (((0,), (0,)), ((), ()))

    def chunk_group(gi, carry):
        starts = [pl.multiple_of((gi * GLA_GROUP + j) * c, c) for j in range(GLA_GROUP)]
        rows = [pl.ds(r0, c) for r0 in starts]
        qts = [qt_s[r, :] for r in rows]
        vs = [v_ref[r, :] for r in rows]
        attn = [lax.dot_general(qt, kt_s[r, :], nt_dims, preferred_element_type=F32)
                for qt, r in zip(qts, rows)]
        cs_t = [lax.dot_general(v, kd_s[r, :], tn_dims, preferred_element_type=F32)
                for v, r in zip(vs, rows)]
        attn = [jnp.where(causal, a, 0.0).astype(BF16) for a in attn]
        o_intra = [jnp.dot(a, v, preferred_element_type=F32) for a, v in zip(attn, vs)]
        st = st_s[...]
        states = []
        for r0, cs in zip(starts, cs_t):
            states.append(st.astype(BF16))
            st = st * dec_s[pl.ds(r0, F32_ROWS), :][0:1, :] + cs
        st_s[...] = st
        o_inter = [lax.dot_general(qt, sb, nt_dims, preferred_element_type=F32)
                   for qt, sb in zip(qts, states)]
        for r, a, b_ in zip(rows, o_intra, o_inter):
            o_s[r, :] = a + b_
        return carry

    lax.fori_loop(0, n_chunks // GLA_GROUP, chunk_group, 0)

    o = o_s[...]
    ms = jnp.mean(o * o, axis=-1, keepdims=True)
    y = o * lax.rsqrt(ms + RMS_EPS) * gn_ref[...]
    r = r_ref[...].astype(F32)
    o_ref[...] = (y * (r * jax.nn.sigmoid(r))).astype(o_ref.dtype)


def _gla(pa, log_g, onorm_g):
    s = SEQ
    return pl.pallas_call(
        _gla_kernel,
        grid=(BATCH, GLA_HEADS),
        in_specs=[
            pl.BlockSpec((s, GLA_DK), lambda b, h: (b, PA_QA // GLA_DK + h)),
            pl.BlockSpec((s, GLA_DK), lambda b, h: (b, PA_KA // GLA_DK + h)),
            pl.BlockSpec((s, GLA_DV), lambda b, h: (b, PA_VA // GLA_DV + h)),
            pl.BlockSpec((s, GLA_DV), lambda b, h: (b, PA_RA // GLA_DV + h)),
            pl.BlockSpec((s, GLA_DK), lambda b, h: (b, h)),
            pl.BlockSpec((1, GLA_DV), lambda b, h: (0, 0)),
        ],
        out_specs=pl.BlockSpec((s, GLA_DV), lambda b, h: (b, h)),
        out_shape=jax.ShapeDtypeStruct((M_TOK, VA_W), BF16),
        scratch_shapes=[
            pltpu.VMEM((s, GLA_DK), BF16),
            pltpu.VMEM((s, GLA_DK), BF16),
            pltpu.VMEM((s, GLA_DK), BF16),
            pltpu.VMEM((s, GLA_DK), F32),
            pltpu.VMEM((GLA_DV, GLA_DK), F32),
            pltpu.VMEM((s, GLA_DV), F32),
        ],
        compiler_params=_cparams(("parallel", "parallel")),
        name="gla_mixer",
    )(pa, pa, pa, pa, log_g, onorm_g.reshape(1, GLA_DV))


def _t5_bucket(dist):
    max_exact = REL_BUCKETS // 2
    safe = np.maximum(dist, 1)
    large = max_exact + (np.log(safe / max_exact) / np.log(REL_MAX_DIST / max_exact)
                         * (REL_BUCKETS - max_exact)).astype(np.int64)
    large = np.minimum(large, REL_BUCKETS - 1)
    return np.where(dist < max_exact, dist, large).astype(np.int32)


ATT_B3, ATT_B2_FIRST, ATT_B2_REST, ATT_B1_FIRST, ATT_B1_REST = range(5)
B2_SLABS, B2_Q_ROWS = 4, 32
B1_SLABS, B1_Q_ROWS = 16, 8
MASKED_BUCKET = -1


def _attention_tables():
    far = 10 ** 6

    def slab_tokens(n_slabs, rows, first_pos, class_step):
        pos = first_pos + np.arange(rows)
        return (N_CLASS * pos[None, :] + class_step * np.arange(n_slabs)[:, None]).reshape(-1)

    specs = []
    a = np.arange(CLASS_LEN)
    specs.append((N_CLASS * a, np.concatenate([N_CLASS * a, np.full(CLASS_LEN, far)]), DIL_PATTERNS[2][0]))
    for first in (True, False):
        tq = slab_tokens(B2_SLABS, B2_Q_ROWS, 0, 4)
        tk = slab_tokens(B2_SLABS, 2 * B2_Q_ROWS, 0 if first else -B2_Q_ROWS, 4)
        specs.append((tq, tk, DIL_PATTERNS[1][0]))
    for first in (True, False):
        tq = slab_tokens(B1_SLABS, B1_Q_ROWS, 0, 1)
        tk = slab_tokens(B1_SLABS, 2 * B1_Q_ROWS, 0 if first else -B1_Q_ROWS, 1)
        specs.append((tq, tk, DIL_PATTERNS[0][0]))
    buckets = []
    for tq, tk, window in specs:
        dist = tq[:, None] - tk[None, :]
        in_band = (dist >= 0) & (dist <= window)
        buckets.append(np.where(in_band, _t5_bucket(np.clip(dist, 0, None)), MASKED_BUCKET))
    return np.stack(buckets).astype(np.int32)


def _bias_kernel(rb_ref, bk_ref, o_ref):
    bk = bk_ref[...]
    for h in range(DIL_HEADS):
        acc = jnp.full(bk.shape, -jnp.inf, F32)
        for bucket in range(REL_BUCKETS):
            acc = jnp.where(bk == bucket, rb_ref[bucket, h] * LOG2E, acc)
        o_ref[h] = acc


def _bias_tables(rel_bias, buckets):
    nt, q, q2 = buckets.shape
    return pl.pallas_call(
        _bias_kernel,
        grid=(nt,),
        in_specs=[pl.BlockSpec(memory_space=pltpu.SMEM),
                  pl.BlockSpec((None, q, q2), lambda i: (i, 0, 0))],
        out_specs=pl.BlockSpec((DIL_HEADS, None, q, q2), lambda i: (0, i, 0, 0)),
        out_shape=jax.ShapeDtypeStruct((DIL_HEADS, nt, q, q2), F32),
        compiler_params=_cparams(("parallel",)),
        name="t5_bias_tables",
    )(rel_bias, buckets)


def _gather(ref, slabs):
    return jnp.concatenate([ref[pl.ds(s, n), :] for s, n in slabs], axis=0)


def _scatter(ref, slabs, val):
    off = 0
    for s, n in slabs:
        ref[pl.ds(s, n), :] = val[off:off + n]
        off += n


def _attn_kernel(q_ref, k_ref, v_ref, bias_ref, o_ref,
                 q32_s, k32_s, v32_s, acc_s, m_s, l_s):
    qb = DIL_BLOCK

    def run_group(blocks, init):
        olds = [None if init else (_gather(m_s, sl), _gather(l_s, sl), _gather(acc_s, sl))
                for *_, sl in blocks]
        logits = [lax.dot_general(q, kw, (((1,), (1,)), ((), ())), preferred_element_type=F32)
                  for q, kw, *_ in blocks]
        probs, stats = [], []
        for s, (_, _, _, tab, nk, _), old in zip(logits, blocks, olds):
            s = s + bias_ref[tab, :, 0:nk]
            m_blk = jnp.broadcast_to(jnp.max(s, axis=-1, keepdims=True), (qb, LANE))
            if init:
                m_new, alpha = m_blk, None
            else:
                m_new = jnp.maximum(old[0], m_blk)
                alpha = jnp.exp2(old[0] - m_new)
            p = jnp.exp2(s - jnp.concatenate([m_new] * (nk // LANE), axis=1))
            l_blk = jnp.broadcast_to(jnp.sum(p, axis=-1, keepdims=True), (qb, LANE))
            probs.append(p.astype(BF16))
            stats.append((m_new, alpha, l_blk))
        pvs = [jnp.dot(p, vw, preferred_element_type=F32) for p, (_, _, vw, *_) in zip(probs, blocks)]
        for pv, (m_new, alpha, l_blk), old, (*_, sl) in zip(pvs, stats, olds, blocks):
            _scatter(m_s, sl, m_new)
            _scatter(l_s, sl, l_blk if init else alpha * old[1] + l_blk)
            _scatter(acc_s, sl, pv if init else alpha * old[2] + pv)

    def b3_body(i, carry):
        blocks = []
        for jj in range(ATT_GROUP):
            r0 = pl.multiple_of((i * ATT_GROUP + jj) * CLASS_LEN, CLASS_LEN)
            rows = pl.ds(r0, CLASS_LEN)
            blocks.append((q_ref[rows, :], k_ref[rows, :], v_ref[rows, :], ATT_B3, CLASS_LEN,
                           [(r0, CLASS_LEN)]))
        run_group(blocks, init=True)
        return carry

    lax.fori_loop(0, N_CLASS // ATT_GROUP, b3_body, 0)

    n_res = DIL_PATTERNS[1][1]
    assert ATT_GROUP % n_res == 0

    def b2_body(i, carry):
        blocks = []
        for jj in range(ATT_GROUP):
            e = jj % n_res
            kk = i * (ATT_GROUP // n_res) + jj // n_res
            q0 = pl.multiple_of(kk * B2_Q_ROWS, B2_Q_ROWS)
            k0 = pl.multiple_of(jnp.maximum(q0 - B2_Q_ROWS, 0), B2_Q_ROWS)
            tab = jnp.where(kk == 0, ATT_B2_FIRST, ATT_B2_REST)
            bases = [(n_res * c + e) * CLASS_LEN for c in range(B2_SLABS)]
            q_slabs = [(pl.multiple_of(base + q0, B2_Q_ROWS), B2_Q_ROWS) for base in bases]
            k_slabs = [(pl.multiple_of(base + k0, B2_Q_ROWS), 2 * B2_Q_ROWS) for base in bases]
            blocks.append((_gather(q_ref, q_slabs), _gather(k_ref, k_slabs), _gather(v_ref, k_slabs),
                           tab, 2 * qb, q_slabs))
        run_group(blocks, init=False)
        return carry

    lax.fori_loop(0, n_res * (CLASS_LEN // B2_Q_ROWS) // ATT_GROUP, b2_body, 0)

    q32_s[...] = q_ref[...].astype(F32)
    k32_s[...] = k_ref[...].astype(F32)
    v32_s[...] = v_ref[...].astype(F32)

    def b1_body(i, carry):
        blocks = []
        for jj in range(ATT_GROUP):
            kk = i * ATT_GROUP + jj
            q0 = pl.multiple_of(kk * B1_Q_ROWS, B1_Q_ROWS)
            k0 = pl.multiple_of(jnp.maximum(q0 - B1_Q_ROWS, 0), B1_Q_ROWS)
            tab = jnp.where(kk == 0, ATT_B1_FIRST, ATT_B1_REST)
            q_slabs = [(pl.multiple_of(u * CLASS_LEN + q0, B1_Q_ROWS), B1_Q_ROWS)
                       for u in range(B1_SLABS)]
            k_slabs = [(pl.multiple_of(u * CLASS_LEN + k0, B1_Q_ROWS), 2 * B1_Q_ROWS)
                       for u in range(B1_SLABS)]
            blocks.append((_gather(q32_s, q_slabs).astype(BF16), _gather(k32_s, k_slabs).astype(BF16),
                           _gather(v32_s, k_slabs).astype(BF16), tab, 2 * qb, q_slabs))
        run_group(blocks, init=False)
        return carry

    lax.fori_loop(0, CLASS_LEN // B1_Q_ROWS // ATT_GROUP, b1_body, 0)

    o_ref[...] = (acc_s[...] / l_s[...]).astype(o_ref.dtype)


def _attention(qk_p, v_p, bias_tabs):
    s = SEQ
    nt = bias_tabs.shape[1]
    heads = DIL_HEADS
    return pl.pallas_call(
        _attn_kernel,
        grid=(BATCH, heads),
        in_specs=[
            pl.BlockSpec((s, DIL_DH), lambda b, h: (b, h)),
            pl.BlockSpec((s, DIL_DH), lambda b, h: (b, heads + h)),
            pl.BlockSpec((s, DIL_DH), lambda b, h: (b, h)),
            pl.BlockSpec((None, nt, DIL_BLOCK, 2 * DIL_BLOCK), lambda b, h: (h, 0, 0, 0)),
        ],
        out_specs=pl.BlockSpec((s, DIL_DH), lambda b, h: (b, h)),
        out_shape=jax.ShapeDtypeStruct((M_TOK, DIL_W), BF16),
        scratch_shapes=[
            pltpu.VMEM((s, DIL_DH), F32),
            pltpu.VMEM((s, DIL_DH), F32),
            pltpu.VMEM((s, DIL_DH), F32),
            pltpu.VMEM((s, DIL_DH), F32),
            pltpu.VMEM((s, LANE), F32),
            pltpu.VMEM((s, LANE), F32),
        ],
        compiler_params=_cparams(("parallel", "parallel")),
        name="dilated_attention",
    )(qk_p, qk_p, v_p, bias_tabs)


def _to_class_order(a):
    w = a.shape[1]
    return a.reshape(BATCH, CLASS_LEN, N_CLASS, w).transpose(0, 2, 1, 3).reshape(M_TOK, w)


def _to_token_order(a):
    w = a.shape[1]
    return a.reshape(BATCH, N_CLASS, CLASS_LEN, w).transpose(0, 2, 1, 3).reshape(M_TOK, w)


def _outproj_kernel(oa_ref, ob_ref, x_ref, w_ref, g_ref, x1_ref, h2_ref):
    oa, ob = oa_ref[...], ob_ref[...]
    d = x_ref.shape[1]
    ssq = jnp.zeros((x_ref.shape[0], 1), F32)
    for n in range(d // OUT_CHUNK):
        cs = slice(n * OUT_CHUNK, (n + 1) * OUT_CHUNK)
        acc = jnp.dot(oa, w_ref[0:VA_W, cs], preferred_element_type=F32)
        acc = acc + jnp.dot(ob, w_ref[VA_W:, cs], preferred_element_type=F32)
        x1 = x_ref[:, cs] + acc
        x1_ref[:, cs] = x1
        ssq = ssq + jnp.sum(x1 * x1, axis=-1, keepdims=True)
    scale = lax.rsqrt(ssq * (1.0 / d) + RMS_EPS)
    h2_ref[...] = (x1_ref[...] * scale * g_ref[...]).astype(h2_ref.dtype)


def _outproj(out_a, out_b, x, w_out, layer, norm2_g):
    m, d = x.shape
    tm = TM_OUT
    row = lambda w: pl.BlockSpec((tm, w), lambda i: (i, 0))
    return pl.pallas_call(
        _outproj_kernel,
        grid=(m // tm,),
        in_specs=[row(VA_W), row(DIL_W), row(d),
                  pl.BlockSpec((None, d, d), lambda i: (layer, 0, 0)),
                  pl.BlockSpec((1, d), lambda i: (0, 0))],
        out_specs=[row(d), row(d)],
        out_shape=[jax.ShapeDtypeStruct((m, d), F32), jax.ShapeDtypeStruct((m, d), BF16)],
        compiler_params=_cparams(("parallel",)),
        name="outproj_norm2",
    )(out_a, out_b, x, w_out, norm2_g.reshape(1, d))


def _ffn_kernel(h_ref, x_ref, wg_ref, wu_ref, wd_ref, o_ref):
    j = pl.program_id(1)

    @pl.when(j == 0)
    def _():
        o_ref[...] = x_ref[...]

    h = h_ref[...]
    gate = jnp.dot(h, wg_ref[...].astype(BF16), preferred_element_type=F32)
    up = jnp.dot(h, wu_ref[...].astype(BF16), preferred_element_type=F32)
    act = (gate * jax.nn.sigmoid(gate) * up).astype(BF16)
    o_ref[...] += jnp.dot(act, wd_ref[...].astype(BF16), preferred_element_type=F32)


def _ffn(h2, x1, w_gate, w_up, w_down, layer):
    m, d = x1.shape
    f = w_gate.shape[2]
    tm, tf = TM_FFN, TF_FFN
    return pl.pallas_call(
        _ffn_kernel,
        grid=(m // tm, f // tf),
        in_specs=[pl.BlockSpec((tm, d), lambda i, j: (i, 0)),
                  pl.BlockSpec((tm, d), lambda i, j: (i, 0)),
                  pl.BlockSpec((None, d, tf), lambda i, j: (layer, 0, j)),
                  pl.BlockSpec((None, d, tf), lambda i, j: (layer, 0, j)),
                  pl.BlockSpec((None, tf, d), lambda i, j: (layer, j, 0))],
        out_specs=pl.BlockSpec((tm, d), lambda i, j: (i, 0)),
        out_shape=jax.ShapeDtypeStruct((m, d), F32),
        compiler_params=_cparams(("parallel", "arbitrary")),
        name="swiglu_ffn",
    )(h2, x1, w_gate, w_up, w_down)


def kernel(x, norm1_g, w_in, gla_gate_w2, gla_gate_b, gla_onorm_g, q_norm_g, k_norm_g, rel_bias,
           w_out, norm2_g, w_gate, w_up, w_down):
    bsz, s, d = x.shape
    assert (bsz, s, d) == (BATCH, SEQ, D_MODEL)
    bias_tabs = _bias_tables(rel_bias.astype(F32), jnp.asarray(_attention_tables()))
    w_out_b = w_out.astype(BF16)
    w_in_t = jnp.swapaxes(w_in, 1, 2)

    xf = x.reshape(M_TOK, D_MODEL)
    for l in range(DEPTH):
        w_gate2 = jnp.pad(gla_gate_w2[l], ((0, GATE_PAD - GLA_GATE_RANK), (0, 0))).astype(BF16)
        qk_gain = jnp.concatenate([jnp.tile(q_norm_g[l], DIL_HEADS) * (DIL_DH ** -0.5 * LOG2E),
                                   jnp.tile(k_norm_g[l], DIL_HEADS)]).reshape(1, 2 * DIL_W).astype(F32)

        h = _rmsnorm(xf, norm1_g[l])
        pa = _proj_wt(h, w_in_t, l, 0, PA_W, name="proj_gla")
        log_g = _gate(h, w_in_t, l, w_gate2, gla_gate_b[l].reshape(1, QA_W).astype(F32))
        out_a = _gla(pa, log_g, gla_onorm_g[l])

        h_p = _to_class_order(h)
        qk_p = _proj_wt(h_p, w_in_t, l, PB_OFF, 2 * DIL_W, gain=qk_gain, name="proj_qknorm")
        v_p = _proj_wt(h_p, w_in_t, l, PB_OFF + 2 * DIL_W, DIL_W, name="proj_v")
        out_b = _to_token_order(_attention(qk_p, v_p, bias_tabs))

        x1, h2 = _outproj(out_a, out_b, xf, w_out_b, l, norm2_g[l])
        xf = _ffn(h2, x1, w_gate, w_up, w_down, l)
    return xf.reshape(bsz, s, d)
```

```python
import functools
import math

import numpy as np
import jax
import jax.numpy as jnp
from jax import lax
from jax.experimental import pallas as pl
from jax.experimental.pallas import tpu as pltpu

D_MODEL = 2048
BATCH = 4
SEQ = 2048
DEPTH = 2
M_TOK = BATCH * SEQ

GLA_HEADS = 4
GLA_DV = 256
GLA_DK = 128
GLA_GATE_RANK = 16
GLA_GATE_TAU = 16.0
GLA_CHUNK = 64
DIL_HEADS = 8
DIL_DH = 128
DIL_PATTERNS = ((128, 1), (512, 4), (2048, 16))
DIL_BLOCK = 128
REL_BUCKETS = 32
REL_MAX_DIST = 2048
FFN_HIDDEN = 5632
RMS_EPS = 1e-6

QA_W = GLA_HEADS * GLA_DK
VA_W = GLA_HEADS * GLA_DV
DIL_W = DIL_HEADS * DIL_DH
PA_QA, PA_KA, PA_VA, PA_RA = 0, QA_W, 2 * QA_W, 2 * QA_W + VA_W
PA_W = PA_RA + VA_W
GA_OFF = PA_W
PB_OFF = GA_OFF + GLA_GATE_RANK
PB_W = 3 * DIL_W
LANE = 128
MXU_N = 256
F32_ROWS = 8
GATE_PAD = LANE

N_CLASS = 16
CLASS_LEN = SEQ // N_CLASS

BF16 = jnp.bfloat16
F32 = jnp.float32

VMEM_LIMIT = 56 * 1024 * 1024

TM_NORM = 512
TM_PROJ, TN_PROJ = 1024, 1024
TM_OUT, OUT_CHUNK = 512, 512
TM_FFN, TF_FFN = 1024, 256
GLA_GROUP = 16
ATT_GROUP = 16
LOG2E = math.log2(math.e)


def _cparams(sem):
    return pltpu.CompilerParams(dimension_semantics=sem, vmem_limit_bytes=VMEM_LIMIT)


NT_DIMS = (((1,), (1,)), ((), ()))


def _norm_gate_kernel(x_ref, g_ref, w1_ref, w2_ref, b_ref, h_ref, lg_ref):
    x = x_ref[...]
    ms = jnp.mean(x * x, axis=-1, keepdims=True)
    h = (x * lax.rsqrt(ms + RMS_EPS) * g_ref[...]).astype(BF16)
    h_ref[...] = h
    w1 = w1_ref[...].astype(BF16)
    w1 = jnp.concatenate([w1, jnp.zeros((GATE_PAD - w1.shape[0], w1.shape[1]), BF16)], axis=0)
    ga = lax.dot_general(h, w1, NT_DIMS, preferred_element_type=F32)
    pre = jnp.dot(ga.astype(BF16), w2_ref[...], preferred_element_type=F32) + b_ref[...]
    lg_ref[...] = jax.nn.log_sigmoid(pre) * (LOG2E / GLA_GATE_TAU)


def _norm_gate(x, g, w_in_t, layer, w2, b):
    m, d = x.shape
    n = w2.shape[1]
    assert GA_OFF % GLA_GATE_RANK == 0
    return pl.pallas_call(
        _norm_gate_kernel,
        grid=(m // TM_NORM,),
        in_specs=[pl.BlockSpec((TM_NORM, d), lambda i: (i, 0)),
                  pl.BlockSpec((1, d), lambda i: (0, 0)),
                  pl.BlockSpec((None, GLA_GATE_RANK, d), lambda i: (layer, GA_OFF // GLA_GATE_RANK, 0)),
                  pl.BlockSpec((GATE_PAD, n), lambda i: (0, 0)),
                  pl.BlockSpec((1, n), lambda i: (0, 0))],
        out_specs=[pl.BlockSpec((TM_NORM, d), lambda i: (i, 0)),
                   pl.BlockSpec((TM_NORM, n), lambda i: (i, 0))],
        out_shape=[jax.ShapeDtypeStruct((m, d), BF16), jax.ShapeDtypeStruct((m, n), F32)],
        compiler_params=_cparams(("parallel",)),
        name="rmsnorm_gate",
    )(x, g.reshape(1, d), w_in_t, w2, b)


def _proj_wt_kernel(h_ref, w_ref, *rest, qk_norm):
    g_ref = rest[0] if qk_norm else None
    o_ref, wb_s = rest[-2:]

    @pl.when(pl.program_id(1) == 0)
    def _():
        wb_s[...] = w_ref[0].astype(BF16)

    h = h_ref[...]
    if not qk_norm:
        o_ref[...] = lax.dot_general(h, wb_s[...], NT_DIMS, preferred_element_type=F32).astype(o_ref.dtype)
        return
    for n in range(o_ref.shape[1] // MXU_N):
        acc = lax.dot_general(h, wb_s[n * MXU_N:(n + 1) * MXU_N, :], NT_DIMS, preferred_element_type=F32)
        for hh in range(MXU_N // DIL_DH):
            cs = slice(n * MXU_N + hh * DIL_DH, n * MXU_N + (hh + 1) * DIL_DH)
            a = acc[:, hh * DIL_DH:(hh + 1) * DIL_DH]
            ms = jnp.mean(a * a, axis=-1, keepdims=True)
            o_ref[:, cs] = (a * lax.rsqrt(ms + RMS_EPS) * g_ref[:, cs]).astype(o_ref.dtype)


def _proj_wt(h, w_in_t, layer, col0, n, gain=None, name="proj"):
    m, k = h.shape
    qk_norm = gain is not None
    in_specs = [pl.BlockSpec((TM_PROJ, k), lambda j, i: (i, 0)),
                pl.BlockSpec((pl.Element(1), pl.Element(TN_PROJ), pl.Element(k)),
                             lambda j, i: (layer, pl.multiple_of(col0 + j * TN_PROJ, F32_ROWS), 0))]
    operands = [h, w_in_t]
    if qk_norm:
        in_specs.append(pl.BlockSpec((1, TN_PROJ), lambda j, i: (0, j)))
        operands.append(gain)
    return pl.pallas_call(
        functools.partial(_proj_wt_kernel, qk_norm=qk_norm),
        grid=(n // TN_PROJ, m // TM_PROJ),
        in_specs=in_specs,
        out_specs=pl.BlockSpec((TM_PROJ, TN_PROJ), lambda j, i: (i, j)),
        out_shape=jax.ShapeDtypeStruct((m, n), BF16),
        scratch_shapes=[pltpu.VMEM((TN_PROJ, k), BF16)],
        compiler_params=_cparams(("parallel", "arbitrary")),
        name=name,
    )(*operands)


def _gla_kernel(q_ref, k_ref, v_ref, r_ref, lg_ref, gn_ref, o_ref,
                qt_s, kt_s, kd_s, dec_s, st_s, o_s):
    s_len, c, n_chunks = SEQ, GLA_CHUNK, SEQ // GLA_CHUNK
    b = lg_ref[...]
    row = lax.broadcasted_iota(jnp.int32, b.shape, 0) & (c - 1)
    shift = 1
    while shift < c:
        b = b + jnp.where(row >= shift, pltpu.roll(b, shift, axis=0), 0.0)
        shift *= 2
    def chunk_last(a):
        a3 = a.reshape(n_chunks, c, GLA_DK)
        return jnp.broadcast_to(a3[:, c - 1:c, :], a3.shape).reshape(s_len, GLA_DK)

    eb = jnp.exp2(b)
    q = q_ref[...].astype(F32) * (GLA_DK ** -0.5)
    k = k_ref[...].astype(F32)
    qt_s[...] = (q * eb).astype(BF16)
    kt_s[...] = (k * jnp.exp2(-b)).astype(BF16)
    kd_s[...] = (k * jnp.exp2(chunk_last(b) - b)).astype(BF16)
    dec_s[...] = chunk_last(eb)
    st_s[...] = jnp.zeros_like(st_s)

    ri = lax.broadcasted_iota(jnp.int32, (c, c), 0)
    ci = lax.broadcasted_iota(jnp.int32, (c, c), 1)
    causal = ri >= ci

    nt_dims = (((1,), (1,)), ((), ()))
    tn_dims = (((0,), (0,)), ((), ()))

    def chunk_group(gi, carry):
        starts = [pl.multiple_of((gi * GLA_GROUP + j) * c, c) for j in range(GLA_GROUP)]
        rows = [pl.ds(r0, c) for r0 in starts]
        qts = [qt_s[r, :] for r in rows]
        vs = [v_ref[r, :] for r in rows]
        attn = [lax.dot_general(qt, kt_s[r, :], nt_dims, preferred_element_type=F32)
                for qt, r in zip(qts, rows)]
        cs_t = [lax.dot_general(v, kd_s[r, :], tn_dims, preferred_element_type=F32)
                for v, r in zip(vs, rows)]
        attn = [jnp.where(causal, a, 0.0).astype(BF16) for a in attn]
        o_intra = [jnp.dot(a, v, preferred_element_type=F32) for a, v in zip(attn, vs)]
        st = st_s[...]
        states = []
        for r0, cs in zip(starts, cs_t):
            states.append(st.astype(BF16))
            st = st * dec_s[pl.ds(r0, F32_ROWS), :][0:1, :] + cs
        st_s[...] = st
        o_inter = [lax.dot_general(qt, sb, nt_dims, preferred_element_type=F32)
                   for qt, sb in zip(qts, states)]
        for r, a, b_ in zip(rows, o_intra, o_inter):
            o_s[r, :] = a + b_
        return carry

    lax.fori_loop(0, n_chunks // GLA_GROUP, chunk_group, 0)

    o = o_s[...]
    ms = jnp.mean(o * o, axis=-1, keepdims=True)
    y = o * lax.rsqrt(ms + RMS_EPS) * gn_ref[...]
    r = r_ref[...].astype(F32)
    o_ref[...] = (y * (r * jax.nn.sigmoid(r))).astype(o_ref.dtype)


def _gla(pa, log_g, onorm_g):
    s = SEQ
    return pl.pallas_call(
        _gla_kernel,
        grid=(BATCH, GLA_HEADS),
        in_specs=[
            pl.BlockSpec((s, GLA_DK), lambda b, h: (b, PA_QA // GLA_DK + h)),
            pl.BlockSpec((s, GLA_DK), lambda b, h: (b, PA_KA // GLA_DK + h)),
            pl.BlockSpec((s, GLA_DV), lambda b, h: (b, PA_VA // GLA_DV + h)),
            pl.BlockSpec((s, GLA_DV), lambda b, h: (b, PA_RA // GLA_DV + h)),
            pl.BlockSpec((s, GLA_DK), lambda b, h: (b, h)),
            pl.BlockSpec((1, GLA_DV), lambda b, h: (0, 0)),
        ],
        out_specs=pl.BlockSpec((s, GLA_DV), lambda b, h: (b, h)),
        out_shape=jax.ShapeDtypeStruct((M_TOK, VA_W), BF16),
        scratch_shapes=[
            pltpu.VMEM((s, GLA_DK), BF16),
            pltpu.VMEM((s, GLA_DK), BF16),
            pltpu.VMEM((s, GLA_DK), BF16),
            pltpu.VMEM((s, GLA_DK), F32),
            pltpu.VMEM((GLA_DV, GLA_DK), F32),
            pltpu.VMEM((s, GLA_DV), F32),
        ],
        compiler_params=_cparams(("parallel", "parallel")),
        name="gla_mixer",
    )(pa, pa, pa, pa, log_g, onorm_g.reshape(1, GLA_DV))


def _t5_bucket(dist):
    max_exact = REL_BUCKETS // 2
    safe = np.maximum(dist, 1)
    large = max_exact + (np.log(safe / max_exact) / np.log(REL_MAX_DIST / max_exact)
                         * (REL_BUCKETS - max_exact)).astype(np.int64)
    large = np.minimum(large, REL_BUCKETS - 1)
    return np.where(dist < max_exact, dist, large).astype(np.int32)


ATT_B3, ATT_B2_FIRST, ATT_B2_REST, ATT_B1_FIRST, ATT_B1_REST = range(5)
B2_SLABS, B2_Q_ROWS = 4, 32
B1_SLABS, B1_Q_ROWS = 16, 8
MASKED_BUCKET = -1


def _attention_tables():
    far = 10 ** 6

    def slab_tokens(n_slabs, rows, first_pos, class_step):
        pos = first_pos + np.arange(rows)
        return (N_CLASS * pos[None, :] + class_step * np.arange(n_slabs)[:, None]).reshape(-1)

    specs = []
    a = np.arange(CLASS_LEN)
    specs.append((N_CLASS * a, np.concatenate([N_CLASS * a, np.full(CLASS_LEN, far)]), DIL_PATTERNS[2][0]))
    for first in (True, False):
        tq = slab_tokens(B2_SLABS, B2_Q_ROWS, 0, 4)
        tk = slab_tokens(B2_SLABS, 2 * B2_Q_ROWS, 0 if first else -B2_Q_ROWS, 4)
        specs.append((tq, tk, DIL_PATTERNS[1][0]))
    for first in (True, False):
        tq = slab_tokens(B1_SLABS, B1_Q_ROWS, 0, 1)
        tk = slab_tokens(B1_SLABS, 2 * B1_Q_ROWS, 0 if first else -B1_Q_ROWS, 1)
        specs.append((tq, tk, DIL_PATTERNS[0][0]))
    buckets = []
    for tq, tk, window in specs:
        dist = tq[:, None] - tk[None, :]
        in_band = (dist >= 0) & (dist <= window)
        buckets.append(np.where(in_band, _t5_bucket(np.clip(dist, 0, None)), MASKED_BUCKET))
    return np.stack(buckets).astype(np.int32)


def _bias_kernel(rb_ref, bk_ref, o_ref):
    bk = bk_ref[...]
    for h in range(DIL_HEADS):
        acc = jnp.full(bk.shape, -jnp.inf, F32)
        for bucket in range(REL_BUCKETS):
            acc = jnp.where(bk == bucket, rb_ref[bucket, h] * LOG2E, acc)
        o_ref[h] = acc


def _bias_tables(rel_bias, buckets):
    nt, q, q2 = buckets.shape
    return pl.pallas_call(
        _bias_kernel,
        grid=(nt,),
        in_specs=[pl.BlockSpec(memory_space=pltpu.SMEM),
                  pl.BlockSpec((None, q, q2), lambda i: (i, 0, 0))],
        out_specs=pl.BlockSpec((DIL_HEADS, None, q, q2), lambda i: (0, i, 0, 0)),
        out_shape=jax.ShapeDtypeStruct((DIL_HEADS, nt, q, q2), F32),
        compiler_params=_cparams(("parallel",)),
        name="t5_bias_tables",
    )(rel_bias, buckets)


def _gather(ref, slabs):
    return jnp.concatenate([ref[pl.ds(s, n), :] for s, n in slabs], axis=0)


def _scatter(ref, slabs, val):
    off = 0
    for s, n in slabs:
        ref[pl.ds(s, n), :] = val[off:off + n]
        off += n


def _attn_kernel(q_ref, k_ref, v_ref, bias_ref, o_ref,
                 q32_s, k32_s, v32_s, acc_s, m_s, l_s):
    qb = DIL_BLOCK

    def run_group(blocks, init):
        olds = [None if init else (_gather(m_s, sl), _gather(l_s, sl), _gather(acc_s, sl))
                for *_, sl in blocks]
        logits = [lax.dot_general(q, kw, (((1,), (1,)), ((), ())), preferred_element_type=F32)
                  for q, kw, *_ in blocks]
        probs, stats = [], []
        for s, (_, _, _, tab, nk, _), old in zip(logits, blocks, olds):
            s = s + bias_ref[tab, :, 0:nk]
            m_blk = jnp.broadcast_to(jnp.max(s, axis=-1, keepdims=True), (qb, LANE))
            if init:
                m_new, alpha = m_blk, None
            else:
                m_new = jnp.maximum(old[0], m_blk)
                alpha = jnp.exp2(old[0] - m_new)
            p = jnp.exp2(s - jnp.concatenate([m_new] * (nk // LANE), axis=1))
            l_blk = jnp.broadcast_to(jnp.sum(p, axis=-1, keepdims=True), (qb, LANE))
            probs.append(p.astype(BF16))
            stats.append((m_new, alpha, l_blk))
        pvs = [jnp.dot(p, vw, preferred_element_type=F32) for p, (_, _, vw, *_) in zip(probs, blocks)]
        for pv, (m_new, alpha, l_blk), old, (*_, sl) in zip(pvs, stats, olds, blocks):
            _scatter(m_s, sl, m_new)
            _scatter(l_s, sl, l_blk if init else alpha * old[1] + l_blk)
            _scatter(acc_s, sl, pv if init else alpha * old[2] + pv)

    def b3_body(i, carry):
        blocks = []
        for jj in range(ATT_GROUP):
            r0 = pl.multiple_of((i * ATT_GROUP + jj) * CLASS_LEN, CLASS_LEN)
            rows = pl.ds(r0, CLASS_LEN)
            blocks.append((q_ref[rows, :], k_ref[rows, :], v_ref[rows, :], ATT_B3, CLASS_LEN,
                           [(r0, CLASS_LEN)]))
        run_group(blocks, init=True)
        return carry

    lax.fori_loop(0, N_CLASS // ATT_GROUP, b3_body, 0)

    n_res = DIL_PATTERNS[1][1]
    assert ATT_GROUP % n_res == 0

    def b2_body(i, carry):
        blocks = []
        for jj in range(ATT_GROUP):
            e = jj % n_res
            kk = i * (ATT_GROUP // n_res) + jj // n_res
            q0 = pl.multiple_of(kk * B2_Q_ROWS, B2_Q_ROWS)
            k0 = pl.multiple_of(jnp.maximum(q0 - B2_Q_ROWS, 0), B2_Q_ROWS)
            tab = jnp.where(kk == 0, ATT_B2_FIRST, ATT_B2_REST)
            bases = [(n_res * c + e) * CLASS_LEN for c in range(B2_SLABS)]
            q_slabs = [(pl.multiple_of(base + q0, B2_Q_ROWS), B2_Q_ROWS) for base in bases]
            k_slabs = [(pl.multiple_of(base + k0, B2_Q_ROWS), 2 * B2_Q_ROWS) for base in bases]
            blocks.append((_gather(q_ref, q_slabs), _gather(k_ref, k_slabs), _gather(v_ref, k_slabs),
                           tab, 2 * qb, q_slabs))
        run_group(blocks, init=False)
        return carry

    lax.fori_loop(0, n_res * (CLASS_LEN // B2_Q_ROWS) // ATT_GROUP, b2_body, 0)

    q32_s[...] = q_ref[...].astype(F32)
    k32_s[...] = k_ref[...].astype(F32)
    v32_s[...] = v_ref[...].astype(F32)

    def b1_body(i, carry):
        blocks = []
        for jj in range(ATT_GROUP):
            kk = i * ATT_GROUP + jj
            q0 = pl.multiple_of(kk * B1_Q_ROWS, B1_Q_ROWS)
            k0 = pl.multiple_of(jnp.maximum(q0 - B1_Q_ROWS, 0), B1_Q_ROWS)
            tab = jnp.where(kk == 0, ATT_B1_FIRST, ATT_B1_REST)
            q_slabs = [(pl.multiple_of(u * CLASS_LEN + q0, B1_Q_ROWS), B1_Q_ROWS)
                       for u in range(B1_SLABS)]
            k_slabs = [(pl.multiple_of(u * CLASS_LEN + k0, B1_Q_ROWS), 2 * B1_Q_ROWS)
                       for u in range(B1_SLABS)]
            blocks.append((_gather(q32_s, q_slabs).astype(BF16), _gather(k32_s, k_slabs).astype(BF16),
                           _gather(v32_s, k_slabs).astype(BF16), tab, 2 * qb, q_slabs))
        run_group(blocks, init=False)
        return carry

    lax.fori_loop(0, CLASS_LEN // B1_Q_ROWS // ATT_GROUP, b1_body, 0)

    o_ref[...] = (acc_s[...] / l_s[...]).astype(o_ref.dtype)


def _attention(qk_p, v_p, bias_tabs):
    s = SEQ
    nt = bias_tabs.shape[1]
    heads = DIL_HEADS
    return pl.pallas_call(
        _attn_kernel,
        grid=(BATCH, heads),
        in_specs=[
            pl.BlockSpec((s, DIL_DH), lambda b, h: (b, h)),
            pl.BlockSpec((s, DIL_DH), lambda b, h: (b, heads + h)),
            pl.BlockSpec((s, DIL_DH), lambda b, h: (b, h)),
            pl.BlockSpec((None, nt, DIL_BLOCK, 2 * DIL_BLOCK), lambda b, h: (h, 0, 0, 0)),
        ],
        out_specs=pl.BlockSpec((s, DIL_DH), lambda b, h: (b, h)),
        out_shape=jax.ShapeDtypeStruct((M_TOK, DIL_W), BF16),
        scratch_shapes=[
            pltpu.VMEM((s, DIL_DH), F32),
            pltpu.VMEM((s, DIL_DH), F32),
            pltpu.VMEM((s, DIL_DH), F32),
            pltpu.VMEM((s, DIL_DH), F32),
            pltpu.VMEM((s, LANE), F32),
            pltpu.VMEM((s, LANE), F32),
        ],
        compiler_params=_cparams(("parallel", "parallel")),
        name="dilated_attention",
    )(qk_p, qk_p, v_p, bias_tabs)


def _to_class_order(a):
    w = a.shape[1]
    return a.reshape(BATCH, CLASS_LEN, N_CLASS, w).transpose(0, 2, 1, 3).reshape(M_TOK, w)


def _to_token_order(a):
    w = a.shape[1]
    return a.reshape(BATCH, N_CLASS, CLASS_LEN, w).transpose(0, 2, 1, 3).reshape(M_TOK, w)


def _outproj_kernel(oa_ref, ob_ref, x_ref, w_ref, g_ref, x1_ref, h2_ref):
    oa, ob = oa_ref[...], ob_ref[...]
    d = x_ref.shape[1]
    ssq = jnp.zeros((x_ref.shape[0], 1), F32)
    for n in range(d // OUT_CHUNK):
        cs = slice(n * OUT_CHUNK, (n + 1) * OUT_CHUNK)
        acc = jnp.dot(oa, w_ref[0:VA_W, cs], preferred_element_type=F32)
        acc = acc + jnp.dot(ob, w_ref[VA_W:, cs], preferred_element_type=F32)
        x1 = x_ref[:, cs] + acc
        x1_ref[:, cs] = x1
        ssq = ssq + jnp.sum(x1 * x1, axis=-1, keepdims=True)
    scale = lax.rsqrt(ssq * (1.0 / d) + RMS_EPS)
    h2_ref[...] = (x1_ref[...] * scale * g_ref[...]).astype(h2_ref.dtype)


def _outproj(out_a, out_b, x, w_out, layer, norm2_g):
    m, d = x.shape
    tm = TM_OUT
    row = lambda w: pl.BlockSpec((tm, w), lambda i: (i, 0))
    return pl.pallas_call(
        _outproj_kernel,
        grid=(m // tm,),
        in_specs=[row(VA_W), row(DIL_W), row(d),
                  pl.BlockSpec((None, d, d), lambda i: (layer, 0, 0)),
                  pl.BlockSpec((1, d), lambda i: (0, 0))],
        out_specs=[row(d), row(d)],
        out_shape=[jax.ShapeDtypeStruct((m, d), F32), jax.ShapeDtypeStruct((m, d), BF16)],
        compiler_params=_cparams(("parallel",)),
        name="outproj_norm2",
    )(out_a, out_b, x, w_out, norm2_g.reshape(1, d))


def _ffn_kernel(h_ref, x_ref, wg_ref, wu_ref, wd_ref, o_ref):
    j = pl.program_id(1)

    @pl.when(j == 0)
    def _():
        o_ref[...] = x_ref[...]

    h = h_ref[...]
    gate = jnp.dot(h, wg_ref[...].astype(BF16), preferred_element_type=F32)
    up = jnp.dot(h, wu_ref[...].astype(BF16), preferred_element_type=F32)
    act = (gate * jax.nn.sigmoid(gate) * up).astype(BF16)
    o_ref[...] += jnp.dot(act, wd_ref[...].astype(BF16), preferred_element_type=F32)


def _ffn(h2, x1, w_gate, w_up, w_down, layer):
    m, d = x1.shape
    f = w_gate.shape[2]
    tm, tf = TM_FFN, TF_FFN
    return pl.pallas_call(
        _ffn_kernel,
        grid=(m // tm, f // tf),
        in_specs=[pl.BlockSpec((tm, d), lambda i, j: (i, 0)),
                  pl.BlockSpec((tm, d), lambda i, j: (i, 0)),
                  pl.BlockSpec((None, d, tf), lambda i, j: (layer, 0, j)),
                  pl.BlockSpec((None, d, tf), lambda i, j: (layer, 0, j)),
                  pl.BlockSpec((None, tf, d), lambda i, j: (layer, j, 0))],
        out_specs=pl.BlockSpec((tm, d), lambda i, j: (i, 0)),
        out_shape=jax.ShapeDtypeStruct((m, d), F32),
        compiler_params=_cparams(("parallel", "arbitrary")),
        name="swiglu_ffn",
    )(h2, x1, w_gate, w_up, w_down)


def kernel(x, norm1_g, w_in, gla_gate_w2, gla_gate_b, gla_onorm_g, q_norm_g, k_norm_g, rel_bias,
           w_out, norm2_g, w_gate, w_up, w_down):
    bsz, s, d = x.shape
    assert (bsz, s, d) == (BATCH, SEQ, D_MODEL)
    bias_tabs = _bias_tables(rel_bias.astype(F32), jnp.asarray(_attention_tables()))
    w_out_b = w_out.astype(BF16)
    w_in_t = jnp.swapaxes(w_in, 1, 2)

    xf = x.reshape(M_TOK, D_MODEL)
    for l in range(DEPTH):
        w_gate2 = jnp.pad(gla_gate_w2[l], ((0, GATE_PAD - GLA_GATE_RANK), (0, 0))).astype(BF16)
        qk_gain = jnp.concatenate([jnp.tile(q_norm_g[l], DIL_HEADS) * (DIL_DH ** -0.5 * LOG2E),
                                   jnp.tile(k_norm_g[l], DIL_HEADS)]).reshape(1, 2 * DIL_W).astype(F32)

        h, log_g = _norm_gate(xf, norm1_g[l], w_in_t, l, w_gate2,
                              gla_gate_b[l].reshape(1, QA_W).astype(F32))
        h_p = _to_class_order(h)
        pa = _proj_wt(h, w_in_t, l, 0, PA_W, name="proj_gla")
        h_p, pa = lax.optimization_barrier((h_p, pa))
        qk_p = _proj_wt(h_p, w_in_t, l, PB_OFF, 2 * DIL_W, gain=qk_gain, name="proj_qknorm")
        v_p = _proj_wt(h_p, w_in_t, l, PB_OFF + 2 * DIL_W, DIL_W, name="proj_v")
        out_b_p = _attention(qk_p, v_p, bias_tabs)
        out_b_p, pa = lax.optimization_barrier((out_b_p, pa))
        out_b = _to_token_order(out_b_p)
        out_a = _gla(pa, log_g, gla_onorm_g[l])

        x1, h2 = _outproj(out_a, out_b, xf, w_out_b, l, norm2_g[l])
        xf = _ffn(h2, x1, w_gate, w_up, w_down, l)
    return xf.reshape(bsz, s, d)
```

```python
import functools
import math

import numpy as np
import jax
import jax.numpy as jnp
from jax import lax
from jax.experimental import pallas as pl
from jax.experimental.pallas import tpu as pltpu

D_MODEL = 2048
BATCH = 4
SEQ = 2048
DEPTH = 2
M_TOK = BATCH * SEQ

GLA_HEADS = 4
GLA_DV = 256
GLA_DK = 128
GLA_GATE_RANK = 16
GLA_GATE_TAU = 16.0
GLA_CHUNK = 64
DIL_HEADS = 8
DIL_DH = 128
DIL_PATTERNS = ((128, 1), (512, 4), (2048, 16))
DIL_BLOCK = 128
REL_BUCKETS = 32
REL_MAX_DIST = 2048
FFN_HIDDEN = 5632
RMS_EPS = 1e-6

QA_W = GLA_HEADS * GLA_DK
VA_W = GLA_HEADS * GLA_DV
DIL_W = DIL_HEADS * DIL_DH
PA_QA, PA_KA, PA_VA, PA_RA = 0, QA_W, 2 * QA_W, 2 * QA_W + VA_W
PA_W = PA_RA + VA_W
GA_OFF = PA_W
PB_OFF = GA_OFF + GLA_GATE_RANK
PB_W = 3 * DIL_W
LANE = 128
MXU_N = 256
F32_ROWS = 8
GATE_PAD = LANE

N_CLASS = 16
CLASS_LEN = SEQ // N_CLASS

BF16 = jnp.bfloat16
F32 = jnp.float32

VMEM_LIMIT = 56 * 1024 * 1024

TM_NORM = 512
TM_PROJ, TN_PROJ = 1024, 1024
TM_OUT, OUT_CHUNK = 512, 512
TM_FFN, TF_FFN = 1024, 256
GLA_GROUP = 16
ATT_GROUP = 16
LOG2E = math.log2(math.e)


def _cparams(sem):
    return pltpu.CompilerParams(dimension_semantics=sem, vmem_limit_bytes=VMEM_LIMIT)


NT_DIMS = (((1,), (1,)), ((), ()))


def _norm_gate_kernel(x_ref, g_ref, w1_ref, w2_ref, b_ref, h_ref, lg_ref):
    x = x_ref[...]
    ms = jnp.mean(x * x, axis=-1, keepdims=True)
    h = (x * lax.rsqrt(ms + RMS_EPS) * g_ref[...]).astype(BF16)
    h_ref[...] = h
    w1 = w1_ref[...].astype(BF16)
    w1 = jnp.concatenate([w1, jnp.zeros((GATE_PAD - w1.shape[0], w1.shape[1]), BF16)], axis=0)
    ga = lax.dot_general(h, w1, NT_DIMS, preferred_element_type=F32)
    pre = jnp.dot(ga.astype(BF16), w2_ref[...], preferred_element_type=F32) + b_ref[...]
    lg_ref[...] = jax.nn.log_sigmoid(pre) * (LOG2E / GLA_GATE_TAU)


def _norm_gate(x, g, w_in_t, layer, w2, b):
    m, d = x.shape
    n = w2.shape[1]
    assert GA_OFF % GLA_GATE_RANK == 0
    return pl.pallas_call(
        _norm_gate_kernel,
        grid=(m // TM_NORM,),
        in_specs=[pl.BlockSpec((TM_NORM, d), lambda i: (i, 0)),
                  pl.BlockSpec((1, d), lambda i: (0, 0)),
                  pl.BlockSpec((None, GLA_GATE_RANK, d), lambda i: (layer, GA_OFF // GLA_GATE_RANK, 0)),
                  pl.BlockSpec((GATE_PAD, n), lambda i: (0, 0)),
                  pl.BlockSpec((1, n), lambda i: (0, 0))],
        out_specs=[pl.BlockSpec((TM_NORM, d), lambda i: (i, 0)),
                   pl.BlockSpec((TM_NORM, n), lambda i: (i, 0))],
        out_shape=[jax.ShapeDtypeStruct((m, d), BF16), jax.ShapeDtypeStruct((m, n), F32)],
        compiler_params=_cparams(("parallel",)),
        name="rmsnorm_gate",
    )(x, g.reshape(1, d), w_in_t, w2, b)


def _proj_wt_kernel(h_ref, w_ref, *rest, qk_norm):
    g_ref = rest[0] if qk_norm else None
    o_ref, wb_s = rest[-2:]

    @pl.when(pl.program_id(1) == 0)
    def _():
        wb_s[...] = w_ref[0].astype(BF16)

    h = h_ref[...]
    if not qk_norm:
        o_ref[...] = lax.dot_general(h, wb_s[...], NT_DIMS, preferred_element_type=F32).astype(o_ref.dtype)
        return
    for n in range(o_ref.shape[1] // MXU_N):
        acc = lax.dot_general(h, wb_s[n * MXU_N:(n + 1) * MXU_N, :], NT_DIMS, preferred_element_type=F32)
        for hh in range(MXU_N // DIL_DH):
            cs = slice(n * MXU_N + hh * DIL_DH, n * MXU_N + (hh + 1) * DIL_DH)
            a = acc[:, hh * DIL_DH:(hh + 1) * DIL_DH]
            ms = jnp.mean(a * a, axis=-1, keepdims=True)
            o_ref[:, cs] = (a * lax.rsqrt(ms + RMS_EPS) * g_ref[:, cs]).astype(o_ref.dtype)


def _proj_wt(h, w_in_t, layer, col0, n, gain=None, name="proj"):
    m, k = h.shape
    qk_norm = gain is not None
    in_specs = [pl.BlockSpec((TM_PROJ, k), lambda j, i: (i, 0)),
                pl.BlockSpec((pl.Element(1), pl.Element(TN_PROJ), pl.Element(k)),
                             lambda j, i: (layer, pl.multiple_of(col0 + j * TN_PROJ, F32_ROWS), 0))]
    operands = [h, w_in_t]
    if qk_norm:
        in_specs.append(pl.BlockSpec((1, TN_PROJ), lambda j, i: (0, j)))
        operands.append(gain)
    return pl.pallas_call(
        functools.partial(_proj_wt_kernel, qk_norm=qk_norm),
        grid=(n // TN_PROJ, m // TM_PROJ),
        in_specs=in_specs,
        out_specs=pl.BlockSpec((TM_PROJ, TN_PROJ), lambda j, i: (i, j)),
        out_shape=jax.ShapeDtypeStruct((m, n), BF16),
        scratch_shapes=[pltpu.VMEM((TN_PROJ, k), BF16)],
        compiler_params=_cparams(("parallel", "arbitrary")),
        name=name,
    )(*operands)


def _gla_kernel(q_ref, k_ref, v_ref, r_ref, lg_ref, gn_ref, o_ref,
                qt_s, kt_s, kd_s, dec_s, st_s, o_s):
    s_len, c, n_chunks = SEQ, GLA_CHUNK, SEQ // GLA_CHUNK
    b = lg_ref[...]
    row = lax.broadcasted_iota(jnp.int32, b.shape, 0) & (c - 1)
    shift = 1
    while shift < c:
        b = b + jnp.where(row >= shift, pltpu.roll(b, shift, axis=0), 0.0)
        shift *= 2
    def chunk_last(a):
        a3 = a.reshape(n_chunks, c, GLA_DK)
        return jnp.broadcast_to(a3[:, c - 1:c, :], a3.shape).reshape(s_len, GLA_DK)

    eb = jnp.exp2(b)
    q = q_ref[...].astype(F32) * (GLA_DK ** -0.5)
    k = k_ref[...].astype(F32)
    qt_s[...] = (q * eb).astype(BF16)
    kt_s[...] = (k * jnp.exp2(-b)).astype(BF16)
    kd_s[...] = (k * jnp.exp2(chunk_last(b) - b)).astype(BF16)
    dec_s[...] = chunk_last(eb)
    st_s[...] = jnp.zeros_like(st_s)

    ri = lax.broadcasted_iota(jnp.int32, (c, c), 0)
    ci = lax.broadcasted_iota(jnp.int32, (c, c), 1)
    causal = ri >= ci

    nt_dims = (((1,), (1,)), ((), ()))
    tn_dims = (((0,), (0,)), ((), ()))

    def chunk_group(gi, carry):
        starts = [pl.multiple_of((gi * GLA_GROUP + j) * c, c) for j in range(GLA_GROUP)]
        rows = [pl.ds(r0, c) for r0 in starts]
        qts = [qt_s[r, :] for r in rows]
        vs = [v_ref[r, :] for r in rows]
        attn = [lax.dot_general(qt, kt_s[r, :], nt_dims, preferred_element_type=F32)
                for qt, r in zip(qts, rows)]
        cs_t = [lax.dot_general(v, kd_s[r, :], tn_dims, preferred_element_type=F32)
                for v, r in zip(vs, rows)]
        attn = [jnp.where(causal, a, 0.0).astype(BF16) for a in attn]
        o_intra = [jnp.dot(a, v, preferred_element_type=F32) for a, v in zip(attn, vs)]
        st = st_s[...]
        states = []
        for r0, cs in zip(starts, cs_t):
            states.append(st.astype(BF16))
            st = st * dec_s[pl.ds(r0, F32_ROWS), :][0:1, :] + cs
        st_s[...] = st
        o_inter = [lax.dot_general(qt, sb, nt_dims, preferred_element_type=F32)
                   for qt, sb in zip(qts, states)]
        for r, a, b_ in zip(rows, o_intra, o_inter):
            o_s[r, :] = a + b_
        return carry

    lax.fori_loop(0, n_chunks // GLA_GROUP, chunk_group, 0)

    o = o_s[...]
    ms = jnp.mean(o * o, axis=-1, keepdims=True)
    y = o * lax.rsqrt(ms + RMS_EPS) * gn_ref[...]
    r = r_ref[...].astype(F32)
    o_ref[...] = (y * (r * jax.nn.sigmoid(r))).astype(o_ref.dtype)


def _gla(pa, log_g, onorm_g):
    s = SEQ
    return pl.pallas_call(
        _gla_kernel,
        grid=(BATCH, GLA_HEADS),
        in_specs=[
            pl.BlockSpec((s, GLA_DK), lambda b, h: (b, PA_QA // GLA_DK + h)),
            pl.BlockSpec((s, GLA_DK), lambda b, h: (b, PA_KA // GLA_DK + h)),
            pl.BlockSpec((s, GLA_DV), lambda b, h: (b, PA_VA // GLA_DV + h)),
            pl.BlockSpec((s, GLA_DV), lambda b, h: (b, PA_RA // GLA_DV + h)),
            pl.BlockSpec((s, GLA_DK), lambda b, h: (b, h)),
            pl.BlockSpec((1, GLA_DV), lambda b, h: (0, 0)),
        ],
        out_specs=pl.BlockSpec((s, GLA_DV), lambda b, h: (b, h)),
        out_shape=jax.ShapeDtypeStruct((M_TOK, VA_W), BF16),
        scratch_shapes=[
            pltpu.VMEM((s, GLA_DK), BF16),
            pltpu.VMEM((s, GLA_DK), BF16),
            pltpu.VMEM((s, GLA_DK), BF16),
            pltpu.VMEM((s, GLA_DK), F32),
            pltpu.VMEM((GLA_DV, GLA_DK), F32),
            pltpu.VMEM((s, GLA_DV), F32),
        ],
        compiler_params=_cparams(("parallel", "parallel")),
        name="gla_mixer",
    )(pa, pa, pa, pa, log_g, onorm_g.reshape(1, GLA_DV))


def _t5_bucket(dist):
    max_exact = REL_BUCKETS // 2
    safe = np.maximum(dist, 1)
    large = max_exact + (np.log(safe / max_exact) / np.log(REL_MAX_DIST / max_exact)
                         * (REL_BUCKETS - max_exact)).astype(np.int64)
    large = np.minimum(large, REL_BUCKETS - 1)
    return np.where(dist < max_exact, dist, large).astype(np.int32)


ATT_B3, ATT_B2_FIRST, ATT_B2_REST, ATT_B1_FIRST, ATT_B1_REST = range(5)
B2_SLABS, B2_Q_ROWS = 4, 32
B1_SLABS, B1_Q_ROWS = 16, 8
MASKED_BUCKET = -1


def _attention_tables():
    far = 10 ** 6

    def slab_tokens(n_slabs, rows, first_pos, class_step):
        pos = first_pos + np.arange(rows)
        return (N_CLASS * pos[None, :] + class_step * np.arange(n_slabs)[:, None]).reshape(-1)

    specs = []
    a = np.arange(CLASS_LEN)
    specs.append((N_CLASS * a, np.concatenate([N_CLASS * a, np.full(CLASS_LEN, far)]), DIL_PATTERNS[2][0]))
    for first in (True, False):
        tq = slab_tokens(B2_SLABS, B2_Q_ROWS, 0, 4)
        tk = slab_tokens(B2_SLABS, 2 * B2_Q_ROWS, 0 if first else -B2_Q_ROWS, 4)
        specs.append((tq, tk, DIL_PATTERNS[1][0]))
    for first in (True, False):
        tq = slab_tokens(B1_SLABS, B1_Q_ROWS, 0, 1)
        tk = slab_tokens(B1_SLABS, 2 * B1_Q_ROWS, 0 if first else -B1_Q_ROWS, 1)
        specs.append((tq, tk, DIL_PATTERNS[0][0]))
    buckets = []
    for tq, tk, window in specs:
        dist = tq[:, None] - tk[None, :]
        in_band = (dist >= 0) & (dist <= window)
        buckets.append(np.where(in_band, _t5_bucket(np.clip(dist, 0, None)), MASKED_BUCKET))
    return np.stack(buckets).astype(np.int32)


def _bias_kernel(rb_ref, bk_ref, o_ref):
    bk = bk_ref[...]
    for h in range(DIL_HEADS):
        acc = jnp.full(bk.shape, -jnp.inf, F32)
        for bucket in range(REL_BUCKETS):
            acc = jnp.where(bk == bucket, rb_ref[bucket, h] * LOG2E, acc)
        o_ref[h] = acc


def _bias_tables(rel_bias, buckets):
    nt, q, q2 = buckets.shape
    return pl.pallas_call(
        _bias_kernel,
        grid=(nt,),
        in_specs=[pl.BlockSpec(memory_space=pltpu.SMEM),
                  pl.BlockSpec((None, q, q2), lambda i: (i, 0, 0))],
        out_specs=pl.BlockSpec((DIL_HEADS, None, q, q2), lambda i: (0, i, 0, 0)),
        out_shape=jax.ShapeDtypeStruct((DIL_HEADS, nt, q, q2), F32),
        compiler_params=_cparams(("parallel",)),
        name="t5_bias_tables",
    )(rel_bias, buckets)


def _gather(ref, slabs):
    return jnp.concatenate([ref[pl.ds(s, n), :] for s, n in slabs], axis=0)


def _scatter(ref, slabs, val):
    off = 0
    for s, n in slabs:
        ref[pl.ds(s, n), :] = val[off:off + n]
        off += n


def _attn_kernel(q_ref, k_ref, v_ref, bias_ref, o_ref,
                 q32_s, k32_s, v32_s, acc_s, m_s, l_s):
    qb = DIL_BLOCK

    def run_group(blocks, init):
        olds = [None if init else (_gather(m_s, sl), _gather(l_s, sl), _gather(acc_s, sl))
                for *_, sl in blocks]
        logits = [lax.dot_general(q, kw, (((1,), (1,)), ((), ())), preferred_element_type=F32)
                  for q, kw, *_ in blocks]
        probs, stats = [], []
        for s, (_, _, _, tab, nk, _), old in zip(logits, blocks, olds):
            s = s + bias_ref[tab, :, 0:nk]
            m_blk = jnp.broadcast_to(jnp.max(s, axis=-1, keepdims=True), (qb, LANE))
            if init:
                m_new, alpha = m_blk, None
            else:
                m_new = jnp.maximum(old[0], m_blk)
                alpha = jnp.exp2(old[0] - m_new)
            p = jnp.exp2(s - jnp.concatenate([m_new] * (nk // LANE), axis=1))
            probs.append(p.astype(BF16))
            stats.append((m_new, alpha))
        pvs = [jnp.dot(p, jnp.concatenate([vw, jnp.ones((nk, LANE), BF16)], axis=1),
                       preferred_element_type=F32)
               for p, (_, _, vw, _, nk, _) in zip(probs, blocks)]
        for pvl, (m_new, alpha), old, (*_, sl) in zip(pvs, stats, olds, blocks):
            pv, l_blk = pvl[:, :DIL_DH], pvl[:, DIL_DH:]
            _scatter(m_s, sl, m_new)
            _scatter(l_s, sl, l_blk if init else alpha * old[1] + l_blk)
            _scatter(acc_s, sl, pv if init else alpha * old[2] + pv)

    def b3_body(i, carry):
        blocks = []
        for jj in range(ATT_GROUP):
            r0 = pl.multiple_of((i * ATT_GROUP + jj) * CLASS_LEN, CLASS_LEN)
            rows = pl.ds(r0, CLASS_LEN)
            blocks.append((q_ref[rows, :], k_ref[rows, :], v_ref[rows, :], ATT_B3, CLASS_LEN,
                           [(r0, CLASS_LEN)]))
        run_group(blocks, init=True)
        return carry

    lax.fori_loop(0, N_CLASS // ATT_GROUP, b3_body, 0)

    n_res = DIL_PATTERNS[1][1]
    assert ATT_GROUP % n_res == 0

    def b2_body(i, carry):
        blocks = []
        for jj in range(ATT_GROUP):
            e = jj % n_res
            kk = i * (ATT_GROUP // n_res) + jj // n_res
            q0 = pl.multiple_of(kk * B2_Q_ROWS, B2_Q_ROWS)
            k0 = pl.multiple_of(jnp.maximum(q0 - B2_Q_ROWS, 0), B2_Q_ROWS)
            tab = jnp.where(kk == 0, ATT_B2_FIRST, ATT_B2_REST)
            bases = [(n_res * c + e) * CLASS_LEN for c in range(B2_SLABS)]
            q_slabs = [(pl.multiple_of(base + q0, B2_Q_ROWS), B2_Q_ROWS) for base in bases]
            k_slabs = [(pl.multiple_of(base + k0, B2_Q_ROWS), 2 * B2_Q_ROWS) for base in bases]
            blocks.append((_gather(q_ref, q_slabs), _gather(k_ref, k_slabs), _gather(v_ref, k_slabs),
                           tab, 2 * qb, q_slabs))
        run_group(blocks, init=False)
        return carry

    lax.fori_loop(0, n_res * (CLASS_LEN // B2_Q_ROWS) // ATT_GROUP, b2_body, 0)

    q32_s[...] = q_ref[...].astype(F32)
    k32_s[...] = k_ref[...].astype(F32)
    v32_s[...] = v_ref[...].astype(F32)

    def b1_body(i, carry):
        blocks = []
        for jj in range(ATT_GROUP):
            kk = i * ATT_GROUP + jj
            q0 = pl.multiple_of(kk * B1_Q_ROWS, B1_Q_ROWS)
            k0 = pl.multiple_of(jnp.maximum(q0 - B1_Q_ROWS, 0), B1_Q_ROWS)
            tab = jnp.where(kk == 0, ATT_B1_FIRST, ATT_B1_REST)
            q_slabs = [(pl.multiple_of(u * CLASS_LEN + q0, B1_Q_ROWS), B1_Q_ROWS)
                       for u in range(B1_SLABS)]
            k_slabs = [(pl.multiple_of(u * CLASS_LEN + k0, B1_Q_ROWS), 2 * B1_Q_ROWS)
                       for u in range(B1_SLABS)]
            blocks.append((_gather(q32_s, q_slabs).astype(BF16), _gather(k32_s, k_slabs).astype(BF16),
                           _gather(v32_s, k_slabs).astype(BF16), tab, 2 * qb, q_slabs))
        run_group(blocks, init=False)
        return carry

    lax.fori_loop(0, CLASS_LEN // B1_Q_ROWS // ATT_GROUP, b1_body, 0)

    o_ref[...] = (acc_s[...] / l_s[...]).astype(o_ref.dtype)


def _attention(qk_p, v_p, bias_tabs):
    s = SEQ
    nt = bias_tabs.shape[1]
    heads = DIL_HEADS
    return pl.pallas_call(
        _attn_kernel,
        grid=(BATCH, heads),
        in_specs=[
            pl.BlockSpec((s, DIL_DH), lambda b, h: (b, h)),
            pl.BlockSpec((s, DIL_DH), lambda b, h: (b, heads + h)),
            pl.BlockSpec((s, DIL_DH), lambda b, h: (b, h)),
            pl.BlockSpec((None, nt, DIL_BLOCK, 2 * DIL_BLOCK), lambda b, h: (h, 0, 0, 0)),
        ],
        out_specs=pl.BlockSpec((s, DIL_DH), lambda b, h: (b, h)),
        out_shape=jax.ShapeDtypeStruct((M_TOK, DIL_W), BF16),
        scratch_shapes=[
            pltpu.VMEM((s, DIL_DH), F32),
            pltpu.VMEM((s, DIL_DH), F32),
            pltpu.VMEM((s, DIL_DH), F32),
            pltpu.VMEM((s, DIL_DH), F32),
            pltpu.VMEM((s, LANE), F32),
            pltpu.VMEM((s, LANE), F32),
        ],
        compiler_params=_cparams(("parallel", "parallel")),
        name="dilated_attention",
    )(qk_p, qk_p, v_p, bias_tabs)


def _to_class_order(a):
    w = a.shape[1]
    return a.reshape(BATCH, CLASS_LEN, N_CLASS, w).transpose(0, 2, 1, 3).reshape(M_TOK, w)


def _to_token_order(a):
    w = a.shape[1]
    return a.reshape(BATCH, N_CLASS, CLASS_LEN, w).transpose(0, 2, 1, 3).reshape(M_TOK, w)


def _outproj_kernel(oa_ref, ob_ref, x_ref, w_ref, g_ref, x1_ref, h2_ref, wb_s):
    @pl.when(pl.program_id(0) == 0)
    def _():
        wb_s[...] = w_ref[...].astype(BF16)

    oa, ob = oa_ref[...], ob_ref[...]
    d = x_ref.shape[1]
    ssq = jnp.zeros((x_ref.shape[0], 1), F32)
    for n in range(d // OUT_CHUNK):
        cs = slice(n * OUT_CHUNK, (n + 1) * OUT_CHUNK)
        acc = jnp.dot(oa, wb_s[0:VA_W, cs], preferred_element_type=F32)
        acc = acc + jnp.dot(ob, wb_s[VA_W:, cs], preferred_element_type=F32)
        x1 = x_ref[:, cs] + acc
        x1_ref[:, cs] = x1
        ssq = ssq + jnp.sum(x1 * x1, axis=-1, keepdims=True)
    scale = lax.rsqrt(ssq * (1.0 / d) + RMS_EPS)
    h2_ref[...] = (x1_ref[...] * scale * g_ref[...]).astype(h2_ref.dtype)


def _outproj(out_a, out_b, x, w_out, layer, norm2_g):
    m, d = x.shape
    tm = TM_OUT
    row = lambda w: pl.BlockSpec((tm, w), lambda i: (i, 0))
    return pl.pallas_call(
        _outproj_kernel,
        grid=(m // tm,),
        in_specs=[row(VA_W), row(DIL_W), row(d),
                  pl.BlockSpec((None, d, d), lambda i: (layer, 0, 0), pipeline_mode=pl.Buffered(1)),
                  pl.BlockSpec((1, d), lambda i: (0, 0))],
        out_specs=[row(d), row(d)],
        out_shape=[jax.ShapeDtypeStruct((m, d), F32), jax.ShapeDtypeStruct((m, d), BF16)],
        scratch_shapes=[pltpu.VMEM((d, d), BF16)],
        compiler_params=_cparams(("arbitrary",)),
        name="outproj_norm2",
    )(out_a, out_b, x, w_out, norm2_g.reshape(1, d))


def _ffn_kernel(h_ref, x_ref, wg_ref, wu_ref, wd_ref, o_ref):
    j = pl.program_id(1)

    @pl.when(j == 0)
    def _():
        o_ref[...] = x_ref[...]

    h = h_ref[...]
    gate = jnp.dot(h, wg_ref[...].astype(BF16), preferred_element_type=F32)
    up = jnp.dot(h, wu_ref[...].astype(BF16), preferred_element_type=F32)
    act = (gate * jax.nn.sigmoid(gate) * up).astype(BF16)
    o_ref[...] += jnp.dot(act, wd_ref[...].astype(BF16), preferred_element_type=F32)


def _ffn(h2, x1, w_gate, w_up, w_down, layer):
    m, d = x1.shape
    f = w_gate.shape[2]
    tm, tf = TM_FFN, TF_FFN
    return pl.pallas_call(
        _ffn_kernel,
        grid=(m // tm, f // tf),
        in_specs=[pl.BlockSpec((tm, d), lambda i, j: (i, 0)),
                  pl.BlockSpec((tm, d), lambda i, j: (i, 0)),
                  pl.BlockSpec((None, d, tf), lambda i, j: (layer, 0, j)),
                  pl.BlockSpec((None, d, tf), lambda i, j: (layer, 0, j)),
                  pl.BlockSpec((None, tf, d), lambda i, j: (layer, j, 0))],
        out_specs=pl.BlockSpec((tm, d), lambda i, j: (i, 0)),
        out_shape=jax.ShapeDtypeStruct((m, d), F32),
        compiler_params=_cparams(("parallel", "arbitrary")),
        name="swiglu_ffn",
    )(h2, x1, w_gate, w_up, w_down)


def kernel(x, norm1_g, w_in, gla_gate_w2, gla_gate_b, gla_onorm_g, q_norm_g, k_norm_g, rel_bias,
           w_out, norm2_g, w_gate, w_up, w_down):
    bsz, s, d = x.shape
    assert (bsz, s, d) == (BATCH, SEQ, D_MODEL)
    bias_tabs = _bias_tables(rel_bias.astype(F32), jnp.asarray(_attention_tables()))
    w_in_t = jnp.swapaxes(w_in, 1, 2)

    xf = x.reshape(M_TOK, D_MODEL)
    for l in range(DEPTH):
        w_gate2 = jnp.pad(gla_gate_w2[l], ((0, GATE_PAD - GLA_GATE_RANK), (0, 0))).astype(BF16)
        qk_gain = jnp.concatenate([jnp.tile(q_norm_g[l], DIL_HEADS) * (DIL_DH ** -0.5 * LOG2E),
                                   jnp.tile(k_norm_g[l], DIL_HEADS)]).reshape(1, 2 * DIL_W).astype(F32)

        h, log_g = _norm_gate(xf, norm1_g[l], w_in_t, l, w_gate2,
                              gla_gate_b[l].reshape(1, QA_W).astype(F32))
        h_p = _to_class_order(h)
        pa = _proj_wt(h, w_in_t, l, 0, PA_W, name="proj_gla")
        h_p, pa = lax.optimization_barrier((h_p, pa))
        qk_p = _proj_wt(h_p, w_in_t, l, PB_OFF, 2 * DIL_W, gain=qk_gain, name="proj_qknorm")
        v_p = _proj_wt(h_p, w_in_t, l, PB_OFF + 2 * DIL_W, DIL_W, name="proj_v")
        out_b_p = _attention(qk_p, v_p, bias_tabs)
        out_b_p, pa = lax.optimization_barrier((out_b_p, pa))
        out_b = _to_token_order(out_b_p)
        out_a = _gla(pa, log_g, gla_onorm_g[l])

        x1, h2 = _outproj(out_a, out_b, xf, w_out, l, norm2_g[l])
        xf = _ffn(h2, x1, w_gate, w_up, w_down, l)
    return xf.reshape(bsz, s, d)
```

```python
import functools
import math

import numpy as np
import jax
import jax.numpy as jnp
from jax import lax
from jax.experimental import pallas as pl
from jax.experimental.pallas import tpu as pltpu

D_MODEL = 2048
BATCH = 4
SEQ = 2048
DEPTH = 2
M_TOK = BATCH * SEQ

GLA_HEADS = 4
GLA_DV = 256
GLA_DK = 128
GLA_GATE_RANK = 16
GLA_GATE_TAU = 16.0
GLA_CHUNK = 64
DIL_HEADS = 8
DIL_DH = 128
DIL_PATTERNS = ((128, 1), (512, 4), (2048, 16))
DIL_BLOCK = 128
REL_BUCKETS = 32
REL_MAX_DIST = 2048
FFN_HIDDEN = 5632
RMS_EPS = 1e-6

QA_W = GLA_HEADS * GLA_DK
VA_W = GLA_HEADS * GLA_DV
DIL_W = DIL_HEADS * DIL_DH
PA_QA, PA_KA, PA_VA, PA_RA = 0, QA_W, 2 * QA_W, 2 * QA_W + VA_W
PA_W = PA_RA + VA_W
GA_OFF = PA_W
PB_OFF = GA_OFF + GLA_GATE_RANK
PB_W = 3 * DIL_W
LANE = 128
MXU_N = 256
F32_ROWS = 8
GATE_PAD = LANE

N_CLASS = 16
CLASS_LEN = SEQ // N_CLASS

BF16 = jnp.bfloat16
F32 = jnp.float32

VMEM_LIMIT = 56 * 1024 * 1024

TM_NORM = 512
TM_PROJ, TN_PROJ = 1024, 1024
TM_OUT, OUT_CHUNK = 512, 512
TM_FFN, TF_FFN = 1024, 512
GLA_GROUP = 16
ATT_GROUP = 16
LOG2E = math.log2(math.e)


def _cparams(sem):
    return pltpu.CompilerParams(dimension_semantics=sem, vmem_limit_bytes=VMEM_LIMIT)


NT_DIMS = (((1,), (1,)), ((), ()))


def _norm_gate_kernel(x_ref, g_ref, w1_ref, w2_ref, b_ref, h_ref, lg_ref):
    x = x_ref[...]
    ms = jnp.mean(x * x, axis=-1, keepdims=True)
    h = (x * lax.rsqrt(ms + RMS_EPS) * g_ref[...]).astype(BF16)
    h_ref[...] = h
    w1 = w1_ref[...].astype(BF16)
    w1 = jnp.concatenate([w1, jnp.zeros((GATE_PAD - w1.shape[0], w1.shape[1]), BF16)], axis=0)
    ga = lax.dot_general(h, w1, NT_DIMS, preferred_element_type=F32)
    pre = jnp.dot(ga.astype(BF16), w2_ref[...], preferred_element_type=F32) + b_ref[...]
    lg_ref[...] = jax.nn.log_sigmoid(pre) * (LOG2E / GLA_GATE_TAU)


def _norm_gate(x, g, w_in_t, layer, w2, b):
    m, d = x.shape
    n = w2.shape[1]
    assert GA_OFF % GLA_GATE_RANK == 0
    return pl.pallas_call(
        _norm_gate_kernel,
        grid=(m // TM_NORM,),
        in_specs=[pl.BlockSpec((TM_NORM, d), lambda i: (i, 0)),
                  pl.BlockSpec((1, d), lambda i: (0, 0)),
                  pl.BlockSpec((None, GLA_GATE_RANK, d), lambda i: (layer, GA_OFF // GLA_GATE_RANK, 0)),
                  pl.BlockSpec((GATE_PAD, n), lambda i: (0, 0)),
                  pl.BlockSpec((1, n), lambda i: (0, 0))],
        out_specs=[pl.BlockSpec((TM_NORM, d), lambda i: (i, 0)),
                   pl.BlockSpec((TM_NORM, n), lambda i: (i, 0))],
        out_shape=[jax.ShapeDtypeStruct((m, d), BF16), jax.ShapeDtypeStruct((m, n), F32)],
        compiler_params=_cparams(("parallel",)),
        name="rmsnorm_gate",
    )(x, g.reshape(1, d), w_in_t, w2, b)


def _proj_wt_kernel(h_ref, w_ref, *rest, qk_norm):
    g_ref = rest[0] if qk_norm else None
    o_ref, wb_s = rest[-2:]

    @pl.when(pl.program_id(1) == 0)
    def _():
        wb_s[...] = w_ref[0].astype(BF16)

    h = h_ref[...]
    if not qk_norm:
        o_ref[...] = lax.dot_general(h, wb_s[...], NT_DIMS, preferred_element_type=F32).astype(o_ref.dtype)
        return
    for n in range(o_ref.shape[1] // MXU_N):
        acc = lax.dot_general(h, wb_s[n * MXU_N:(n + 1) * MXU_N, :], NT_DIMS, preferred_element_type=F32)
        for hh in range(MXU_N // DIL_DH):
            cs = slice(n * MXU_N + hh * DIL_DH, n * MXU_N + (hh + 1) * DIL_DH)
            a = acc[:, hh * DIL_DH:(hh + 1) * DIL_DH]
            ms = jnp.mean(a * a, axis=-1, keepdims=True)
            o_ref[:, cs] = (a * lax.rsqrt(ms + RMS_EPS) * g_ref[:, cs]).astype(o_ref.dtype)


def _proj_wt(h, w_in_t, layer, col0, n, gain=None, name="proj"):
    m, k = h.shape
    qk_norm = gain is not None
    in_specs = [pl.BlockSpec((TM_PROJ, k), lambda j, i: (i, 0)),
                pl.BlockSpec((pl.Element(1), pl.Element(TN_PROJ), pl.Element(k)),
                             lambda j, i: (layer, pl.multiple_of(col0 + j * TN_PROJ, F32_ROWS), 0))]
    operands = [h, w_in_t]
    if qk_norm:
        in_specs.append(pl.BlockSpec((1, TN_PROJ), lambda j, i: (0, j)))
        operands.append(gain)
    return pl.pallas_call(
        functools.partial(_proj_wt_kernel, qk_norm=qk_norm),
        grid=(n // TN_PROJ, m // TM_PROJ),
        in_specs=in_specs,
        out_specs=pl.BlockSpec((TM_PROJ, TN_PROJ), lambda j, i: (i, j)),
        out_shape=jax.ShapeDtypeStruct((m, n), BF16),
        scratch_shapes=[pltpu.VMEM((TN_PROJ, k), BF16)],
        compiler_params=_cparams(("parallel", "arbitrary")),
        name=name,
    )(*operands)


def _gla_kernel(q_ref, k_ref, v_ref, r_ref, lg_ref, gn_ref, o_ref,
                qt_s, kt_s, kd_s, dec_s, st_s, o_s):
    s_len, c, n_chunks = SEQ, GLA_CHUNK, SEQ // GLA_CHUNK
    b = lg_ref[...]
    row = lax.broadcasted_iota(jnp.int32, b.shape, 0) & (c - 1)
    shift = 1
    while shift < c:
        b = b + jnp.where(row >= shift, pltpu.roll(b, shift, axis=0), 0.0)
        shift *= 2
    def chunk_last(a):
        a3 = a.reshape(n_chunks, c, GLA_DK)
        return jnp.broadcast_to(a3[:, c - 1:c, :], a3.shape).reshape(s_len, GLA_DK)

    eb = jnp.exp2(b)
    q = q_ref[...].astype(F32) * (GLA_DK ** -0.5)
    k = k_ref[...].astype(F32)
    qt_s[...] = (q * eb).astype(BF16)
    kt_s[...] = (k * jnp.exp2(-b)).astype(BF16)
    kd_s[...] = (k * jnp.exp2(chunk_last(b) - b)).astype(BF16)
    dec_s[...] = chunk_last(eb)
    st_s[...] = jnp.zeros_like(st_s)

    ri = lax.broadcasted_iota(jnp.int32, (c, c), 0)
    ci = lax.broadcasted_iota(jnp.int32, (c, c), 1)
    causal = ri >= ci

    nt_dims = (((1,), (1,)), ((), ()))
    tn_dims = (((0,), (0,)), ((), ()))

    def chunk_group(gi, carry):
        starts = [pl.multiple_of((gi * GLA_GROUP + j) * c, c) for j in range(GLA_GROUP)]
        rows = [pl.ds(r0, c) for r0 in starts]
        qts = [qt_s[r, :] for r in rows]
        vs = [v_ref[r, :] for r in rows]
        attn = [lax.dot_general(qt, kt_s[r, :], nt_dims, preferred_element_type=F32)
                for qt, r in zip(qts, rows)]
        cs_t = [lax.dot_general(v, kd_s[r, :], tn_dims, preferred_element_type=F32)
                for v, r in zip(vs, rows)]
        attn = [jnp.where(causal, a, 0.0).astype(BF16) for a in attn]
        o_intra = [jnp.dot(a, v, preferred_element_type=F32) for a, v in zip(attn, vs)]
        st = st_s[...]
        states = []
        for r0, cs in zip(starts, cs_t):
            states.append(st.astype(BF16))
            st = st * dec_s[pl.ds(r0, F32_ROWS), :][0:1, :] + cs
        st_s[...] = st
        o_inter = [lax.dot_general(qt, sb, nt_dims, preferred_element_type=F32)
                   for qt, sb in zip(qts, states)]
        for r, a, b_ in zip(rows, o_intra, o_inter):
            o_s[r, :] = a + b_
        return carry

    lax.fori_loop(0, n_chunks // GLA_GROUP, chunk_group, 0)

    o = o_s[...]
    ms = jnp.mean(o * o, axis=-1, keepdims=True)
    y = o * lax.rsqrt(ms + RMS_EPS) * gn_ref[...]
    r = r_ref[...].astype(F32)
    o_ref[...] = (y * (r * jax.nn.sigmoid(r))).astype(o_ref.dtype)


def _gla(pa, log_g, onorm_g):
    s = SEQ
    return pl.pallas_call(
        _gla_kernel,
        grid=(BATCH, GLA_HEADS),
        in_specs=[
            pl.BlockSpec((s, GLA_DK), lambda b, h: (b, PA_QA // GLA_DK + h)),
            pl.BlockSpec((s, GLA_DK), lambda b, h: (b, PA_KA // GLA_DK + h)),
            pl.BlockSpec((s, GLA_DV), lambda b, h: (b, PA_VA // GLA_DV + h)),
            pl.BlockSpec((s, GLA_DV), lambda b, h: (b, PA_RA // GLA_DV + h)),
            pl.BlockSpec((s, GLA_DK), lambda b, h: (b, h)),
            pl.BlockSpec((1, GLA_DV), lambda b, h: (0, 0)),
        ],
        out_specs=pl.BlockSpec((s, GLA_DV), lambda b, h: (b, h)),
        out_shape=jax.ShapeDtypeStruct((M_TOK, VA_W), BF16),
        scratch_shapes=[
            pltpu.VMEM((s, GLA_DK), BF16),
            pltpu.VMEM((s, GLA_DK), BF16),
            pltpu.VMEM((s, GLA_DK), BF16),
            pltpu.VMEM((s, GLA_DK), F32),
            pltpu.VMEM((GLA_DV, GLA_DK), F32),
            pltpu.VMEM((s, GLA_DV), F32),
        ],
        compiler_params=_cparams(("parallel", "parallel")),
        name="gla_mixer",
    )(pa, pa, pa, pa, log_g, onorm_g.reshape(1, GLA_DV))


def _t5_bucket(dist):
    max_exact = REL_BUCKETS // 2
    safe = np.maximum(dist, 1)
    large = max_exact + (np.log(safe / max_exact) / np.log(REL_MAX_DIST / max_exact)
                         * (REL_BUCKETS - max_exact)).astype(np.int64)
    large = np.minimum(large, REL_BUCKETS - 1)
    return np.where(dist < max_exact, dist, large).astype(np.int32)


ATT_B3, ATT_B2_FIRST, ATT_B2_REST, ATT_B1_FIRST, ATT_B1_REST = range(5)
B2_SLABS, B2_Q_ROWS = 4, 32
B1_SLABS, B1_Q_ROWS = 16, 8
MASKED_BUCKET = -1


def _attention_tables():
    far = 10 ** 6

    def slab_tokens(n_slabs, rows, first_pos, class_step):
        pos = first_pos + np.arange(rows)
        return (N_CLASS * pos[None, :] + class_step * np.arange(n_slabs)[:, None]).reshape(-1)

    specs = []
    a = np.arange(CLASS_LEN)
    specs.append((N_CLASS * a, np.concatenate([N_CLASS * a, np.full(CLASS_LEN, far)]), DIL_PATTERNS[2][0]))
    for first in (True, False):
        tq = slab_tokens(B2_SLABS, B2_Q_ROWS, 0, 4)
        tk = slab_tokens(B2_SLABS, 2 * B2_Q_ROWS, 0 if first else -B2_Q_ROWS, 4)
        specs.append((tq, tk, DIL_PATTERNS[1][0]))
    for first in (True, False):
        tq = slab_tokens(B1_SLABS, B1_Q_ROWS, 0, 1)
        tk = slab_tokens(B1_SLABS, 2 * B1_Q_ROWS, 0 if first else -B1_Q_ROWS, 1)
        specs.append((tq, tk, DIL_PATTERNS[0][0]))
    buckets = []
    for tq, tk, window in specs:
        dist = tq[:, None] - tk[None, :]
        in_band = (dist >= 0) & (dist <= window)
        buckets.append(np.where(in_band, _t5_bucket(np.clip(dist, 0, None)), MASKED_BUCKET))
    return np.stack(buckets).astype(np.int32)


def _bias_kernel(rb_ref, bk_ref, o_ref):
    bk = bk_ref[...]
    for h in range(DIL_HEADS):
        acc = jnp.full(bk.shape, -jnp.inf, F32)
        for bucket in range(REL_BUCKETS):
            acc = jnp.where(bk == bucket, rb_ref[bucket, h] * LOG2E, acc)
        o_ref[h] = acc


def _bias_tables(rel_bias, buckets):
    nt, q, q2 = buckets.shape
    return pl.pallas_call(
        _bias_kernel,
        grid=(nt,),
        in_specs=[pl.BlockSpec(memory_space=pltpu.SMEM),
                  pl.BlockSpec((None, q, q2), lambda i: (i, 0, 0))],
        out_specs=pl.BlockSpec((DIL_HEADS, None, q, q2), lambda i: (0, i, 0, 0)),
        out_shape=jax.ShapeDtypeStruct((DIL_HEADS, nt, q, q2), F32),
        compiler_params=_cparams(("parallel",)),
        name="t5_bias_tables",
    )(rel_bias, buckets)


def _gather(ref, slabs):
    return jnp.concatenate([ref[pl.ds(s, n), :] for s, n in slabs], axis=0)


def _scatter(ref, slabs, val):
    off = 0
    for s, n in slabs:
        ref[pl.ds(s, n), :] = val[off:off + n]
        off += n


def _attn_kernel(q_ref, k_ref, v_ref, bias_ref, o_ref,
                 q32_s, k32_s, v32_s, acc_s, m_s, l_s):
    qb = DIL_BLOCK

    def run_group(blocks, init):
        olds = [None if init else (_gather(m_s, sl), _gather(l_s, sl), _gather(acc_s, sl))
                for *_, sl in blocks]
        logits = [lax.dot_general(q, kw, (((1,), (1,)), ((), ())), preferred_element_type=F32)
                  for q, kw, *_ in blocks]
        probs, stats = [], []
        for s, (_, _, _, tab, nk, _), old in zip(logits, blocks, olds):
            s = s + bias_ref[tab, :, 0:nk]
            m_blk = jnp.broadcast_to(jnp.max(s, axis=-1, keepdims=True), (qb, LANE))
            if init:
                m_new, alpha = m_blk, None
            else:
                m_new = jnp.maximum(old[0], m_blk)
                alpha = jnp.exp2(old[0] - m_new)
            p = jnp.exp2(s - jnp.concatenate([m_new] * (nk // LANE), axis=1))
            probs.append(p.astype(BF16))
            stats.append((m_new, alpha))
        pvs = [jnp.dot(p, jnp.concatenate([vw, jnp.ones((nk, LANE), BF16)], axis=1),
                       preferred_element_type=F32)
               for p, (_, _, vw, _, nk, _) in zip(probs, blocks)]
        for pvl, (m_new, alpha), old, (*_, sl) in zip(pvs, stats, olds, blocks):
            pv, l_blk = pvl[:, :DIL_DH], pvl[:, DIL_DH:]
            _scatter(m_s, sl, m_new)
            _scatter(l_s, sl, l_blk if init else alpha * old[1] + l_blk)
            _scatter(acc_s, sl, pv if init else alpha * old[2] + pv)

    def b3_body(i, carry):
        blocks = []
        for jj in range(ATT_GROUP):
            r0 = pl.multiple_of((i * ATT_GROUP + jj) * CLASS_LEN, CLASS_LEN)
            rows = pl.ds(r0, CLASS_LEN)
            blocks.append((q_ref[rows, :], k_ref[rows, :], v_ref[rows, :], ATT_B3, CLASS_LEN,
                           [(r0, CLASS_LEN)]))
        run_group(blocks, init=True)
        return carry

    lax.fori_loop(0, N_CLASS // ATT_GROUP, b3_body, 0)

    n_res = DIL_PATTERNS[1][1]
    assert ATT_GROUP % n_res == 0

    def b2_body(i, carry):
        blocks = []
        for jj in range(ATT_GROUP):
            e = jj % n_res
            kk = i * (ATT_GROUP // n_res) + jj // n_res
            q0 = pl.multiple_of(kk * B2_Q_ROWS, B2_Q_ROWS)
            k0 = pl.multiple_of(jnp.maximum(q0 - B2_Q_ROWS, 0), B2_Q_ROWS)
            tab = jnp.where(kk == 0, ATT_B2_FIRST, ATT_B2_REST)
            bases = [(n_res * c + e) * CLASS_LEN for c in range(B2_SLABS)]
            q_slabs = [(pl.multiple_of(base + q0, B2_Q_ROWS), B2_Q_ROWS) for base in bases]
            k_slabs = [(pl.multiple_of(base + k0, B2_Q_ROWS), 2 * B2_Q_ROWS) for base in bases]
            blocks.append((_gather(q_ref, q_slabs), _gather(k_ref, k_slabs), _gather(v_ref, k_slabs),
                           tab, 2 * qb, q_slabs))
        run_group(blocks, init=False)
        return carry

    lax.fori_loop(0, n_res * (CLASS_LEN // B2_Q_ROWS) // ATT_GROUP, b2_body, 0)

    q32_s[...] = q_ref[...].astype(F32)
    k32_s[...] = k_ref[...].astype(F32)
    v32_s[...] = v_ref[...].astype(F32)

    def b1_body(i, carry):
        blocks = []
        for jj in range(ATT_GROUP):
            kk = i * ATT_GROUP + jj
            q0 = pl.multiple_of(kk * B1_Q_ROWS, B1_Q_ROWS)
            k0 = pl.multiple_of(jnp.maximum(q0 - B1_Q_ROWS, 0), B1_Q_ROWS)
            tab = jnp.where(kk == 0, ATT_B1_FIRST, ATT_B1_REST)
            q_slabs = [(pl.multiple_of(u * CLASS_LEN + q0, B1_Q_ROWS), B1_Q_ROWS)
                       for u in range(B1_SLABS)]
            k_slabs = [(pl.multiple_of(u * CLASS_LEN + k0, B1_Q_ROWS), 2 * B1_Q_ROWS)
                       for u in range(B1_SLABS)]
            blocks.append((_gather(q32_s, q_slabs).astype(BF16), _gather(k32_s, k_slabs).astype(BF16),
                           _gather(v32_s, k_slabs).astype(BF16), tab, 2 * qb, q_slabs))
        run_group(blocks, init=False)
        return carry

    lax.fori_loop(0, CLASS_LEN // B1_Q_ROWS // ATT_GROUP, b1_body, 0)

    o_ref[...] = (acc_s[...] / l_s[...]).astype(o_ref.dtype)


def _attention(qk_p, v_p, bias_tabs):
    s = SEQ
    nt = bias_tabs.shape[1]
    heads = DIL_HEADS
    return pl.pallas_call(
        _attn_kernel,
        grid=(BATCH, heads),
        in_specs=[
            pl.BlockSpec((s, DIL_DH), lambda b, h: (b, h)),
            pl.BlockSpec((s, DIL_DH), lambda b, h: (b, heads + h)),
            pl.BlockSpec((s, DIL_DH), lambda b, h: (b, h)),
            pl.BlockSpec((None, nt, DIL_BLOCK, 2 * DIL_BLOCK), lambda b, h: (h, 0, 0, 0)),
        ],
        out_specs=pl.BlockSpec((s, DIL_DH), lambda b, h: (b, h)),
        out_shape=jax.ShapeDtypeStruct((M_TOK, DIL_W), BF16),
        scratch_shapes=[
            pltpu.VMEM((s, DIL_DH), F32),
            pltpu.VMEM((s, DIL_DH), F32),
            pltpu.VMEM((s, DIL_DH), F32),
            pltpu.VMEM((s, DIL_DH), F32),
            pltpu.VMEM((s, LANE), F32),
            pltpu.VMEM((s, LANE), F32),
        ],
        compiler_params=_cparams(("parallel", "parallel")),
        name="dilated_attention",
    )(qk_p, qk_p, v_p, bias_tabs)


def _to_class_order(a):
    w = a.shape[1]
    return a.reshape(BATCH, CLASS_LEN, N_CLASS, w).transpose(0, 2, 1, 3).reshape(M_TOK, w)


def _to_token_order(a):
    w = a.shape[1]
    return a.reshape(BATCH, N_CLASS, CLASS_LEN, w).transpose(0, 2, 1, 3).reshape(M_TOK, w)


def _outproj_kernel(oa_ref, ob_ref, x_ref, w_ref, g_ref, x1_ref, h2_ref, wb_s):
    @pl.when(pl.program_id(0) == 0)
    def _():
        wb_s[...] = w_ref[...].astype(BF16)

    oa, ob = oa_ref[...], ob_ref[...]
    d = x_ref.shape[1]
    ssq = jnp.zeros((x_ref.shape[0], 1), F32)
    for n in range(d // OUT_CHUNK):
        cs = slice(n * OUT_CHUNK, (n + 1) * OUT_CHUNK)
        acc = jnp.dot(oa, wb_s[0:VA_W, cs], preferred_element_type=F32)
        acc = acc + jnp.dot(ob, wb_s[VA_W:, cs], preferred_element_type=F32)
        x1 = x_ref[:, cs] + acc
        x1_ref[:, cs] = x1
        ssq = ssq + jnp.sum(x1 * x1, axis=-1, keepdims=True)
    scale = lax.rsqrt(ssq * (1.0 / d) + RMS_EPS)
    h2_ref[...] = (x1_ref[...] * scale * g_ref[...]).astype(h2_ref.dtype)


def _outproj(out_a, out_b, x, w_out, layer, norm2_g):
    m, d = x.shape
    tm = TM_OUT
    row = lambda w: pl.BlockSpec((tm, w), lambda i: (i, 0))
    return pl.pallas_call(
        _outproj_kernel,
        grid=(m // tm,),
        in_specs=[row(VA_W), row(DIL_W), row(d),
                  pl.BlockSpec((None, d, d), lambda i: (layer, 0, 0), pipeline_mode=pl.Buffered(1)),
                  pl.BlockSpec((1, d), lambda i: (0, 0))],
        out_specs=[row(d), row(d)],
        out_shape=[jax.ShapeDtypeStruct((m, d), F32), jax.ShapeDtypeStruct((m, d), BF16)],
        scratch_shapes=[pltpu.VMEM((d, d), BF16)],
        compiler_params=_cparams(("arbitrary",)),
        name="outproj_norm2",
    )(out_a, out_b, x, w_out, norm2_g.reshape(1, d))


def _ffn_kernel(h_ref, x_hbm, wg_ref, wu_ref, wd_ref, o_ref, x_sem):
    i, j = pl.program_id(0), pl.program_id(1)
    tm = o_ref.shape[0]

    def hidden():
        h = h_ref[...]
        gate = jnp.dot(h, wg_ref[...].astype(BF16), preferred_element_type=F32)
        up = jnp.dot(h, wu_ref[...].astype(BF16), preferred_element_type=F32)
        return (gate * jax.nn.sigmoid(gate) * up).astype(BF16)

    def down(act):
        return jnp.dot(act, wd_ref[...].astype(BF16), preferred_element_type=F32)

    @pl.when(j == 0)
    def _():
        rows = pl.ds(pl.multiple_of(i * tm, tm), tm)
        residual = pltpu.make_async_copy(x_hbm.at[rows, :], o_ref, x_sem)
        residual.start()
        act = hidden()
        residual.wait()
        o_ref[...] += down(act)

    @pl.when(j > 0)
    def _():
        o_ref[...] += down(hidden())


def _ffn(h2, x1, w_gate, w_up, w_down, layer):
    m, d = x1.shape
    f = w_gate.shape[2]
    tm, tf = TM_FFN, TF_FFN
    return pl.pallas_call(
        _ffn_kernel,
        grid=(m // tm, f // tf),
        in_specs=[pl.BlockSpec((tm, d), lambda i, j: (i, 0)),
                  pl.BlockSpec(memory_space=pl.ANY),
                  pl.BlockSpec((None, d, tf), lambda i, j: (layer, 0, j)),
                  pl.BlockSpec((None, d, tf), lambda i, j: (layer, 0, j)),
                  pl.BlockSpec((None, tf, d), lambda i, j: (layer, j, 0))],
        out_specs=pl.BlockSpec((tm, d), lambda i, j: (i, 0)),
        out_shape=jax.ShapeDtypeStruct((m, d), F32),
        scratch_shapes=[pltpu.SemaphoreType.DMA(())],
        compiler_params=_cparams(("parallel", "arbitrary")),
        name="swiglu_ffn",
    )(h2, x1, w_gate, w_up, w_down)


def kernel(x, norm1_g, w_in, gla_gate_w2, gla_gate_b, gla_onorm_g, q_norm_g, k_norm_g, rel_bias,
           w_out, norm2_g, w_gate, w_up, w_down):
    bsz, s, d = x.shape
    assert (bsz, s, d) == (BATCH, SEQ, D_MODEL)
    bias_tabs = _bias_tables(rel_bias.astype(F32), jnp.asarray(_attention_tables()))
    w_in_t = jnp.swapaxes(w_in, 1, 2)

    xf = x.reshape(M_TOK, D_MODEL)
    for l in range(DEPTH):
        w_gate2 = jnp.pad(gla_gate_w2[l], ((0, GATE_PAD - GLA_GATE_RANK), (0, 0))).astype(BF16)
        qk_gain = jnp.concatenate([jnp.tile(q_norm_g[l], DIL_HEADS) * (DIL_DH ** -0.5 * LOG2E),
                                   jnp.tile(k_norm_g[l], DIL_HEADS)]).reshape(1, 2 * DIL_W).astype(F32)

        h, log_g = _norm_gate(xf, norm1_g[l], w_in_t, l, w_gate2,
                              gla_gate_b[l].reshape(1, QA_W).astype(F32))
        h_p = _to_class_order(h)
        pa = _proj_wt(h, w_in_t, l, 0, PA_W, name="proj_gla")
        h_p, pa = lax.optimization_barrier((h_p, pa))
        qk_p = _proj_wt(h_p, w_in_t, l, PB_OFF, 2 * DIL_W, gain=qk_gain, name="proj_qknorm")
        v_p = _proj_wt(h_p, w_in_t, l, PB_OFF + 2 * DIL_W, DIL_W, name="proj_v")
        out_b_p = _attention(qk_p, v_p, bias_tabs)
        out_b_p, pa = lax.optimization_barrier((out_b_p, pa))
        out_b = _to_token_order(out_b_p)
        out_a = _gla(pa, log_g, gla_onorm_g[l])

        x1, h2 = _outproj(out_a, out_b, xf, w_out, l, norm2_g[l])
        xf = _ffn(h2, x1, w_gate, w_up, w_down, l)
    return xf.reshape(bsz, s, d)
```

```python
import functools
import math

import numpy as np
import jax
import jax.numpy as jnp
from jax import lax
from jax.experimental import pallas as pl
from jax.experimental.pallas import tpu as pltpu

D_MODEL = 2048
BATCH = 4
SEQ = 2048
DEPTH = 2
M_TOK = BATCH * SEQ

GLA_HEADS = 4
GLA_DV = 256
GLA_DK = 128
GLA_GATE_RANK = 16
GLA_GATE_TAU = 16.0
GLA_CHUNK = 64
DIL_HEADS = 8
DIL_DH = 128
DIL_PATTERNS = ((128, 1), (512, 4), (2048, 16))
DIL_BLOCK = 128
REL_BUCKETS = 32
REL_MAX_DIST = 2048
FFN_HIDDEN = 5632
RMS_EPS = 1e-6

QA_W = GLA_HEADS * GLA_DK
VA_W = GLA_HEADS * GLA_DV
DIL_W = DIL_HEADS * DIL_DH
PA_QA, PA_KA, PA_VA, PA_RA = 0, QA_W, 2 * QA_W, 2 * QA_W + VA_W
PA_W = PA_RA + VA_W
GA_OFF = PA_W
PB_OFF = GA_OFF + GLA_GATE_RANK
PB_W = 3 * DIL_W
LANE = 128
MXU_N = 256
F32_ROWS = 8
GATE_PAD = LANE

N_CLASS = 16
CLASS_LEN = SEQ // N_CLASS

BF16 = jnp.bfloat16
F32 = jnp.float32

VMEM_LIMIT = 56 * 1024 * 1024

TM_NORM = 512
TM_PROJ, TN_PROJ = 1024, 1024
TM_OUT, OUT_CHUNK = 512, 512
TM_FFN, TF_FFN = 1024, 512
GLA_GROUP = 16
ATT_GROUP = 16
LOG2E = math.log2(math.e)


def _cparams(sem):
    return pltpu.CompilerParams(dimension_semantics=sem, vmem_limit_bytes=VMEM_LIMIT)


NT_DIMS = (((1,), (1,)), ((), ()))


def _norm_gate_kernel(x_ref, g_ref, w1_ref, w2_ref, b_ref, h_ref, lg_ref):
    x = x_ref[...]
    ms = jnp.mean(x * x, axis=-1, keepdims=True)
    h = (x * lax.rsqrt(ms + RMS_EPS) * g_ref[...]).astype(BF16)
    h_ref[...] = h
    w1 = w1_ref[...].astype(BF16)
    w1 = jnp.concatenate([w1, jnp.zeros((GATE_PAD - w1.shape[0], w1.shape[1]), BF16)], axis=0)
    ga = lax.dot_general(h, w1, NT_DIMS, preferred_element_type=F32)
    pre = jnp.dot(ga.astype(BF16), w2_ref[...], preferred_element_type=F32) + b_ref[...]
    y = pre * LOG2E
    lg_ref[...] = (jnp.minimum(y, 0.0) - jnp.log2(1.0 + jnp.exp2(-jnp.abs(y)))) * (1.0 / GLA_GATE_TAU)


def _norm_gate(x, g, w_in_t, layer, w2, b):
    m, d = x.shape
    n = w2.shape[1]
    assert GA_OFF % GLA_GATE_RANK == 0
    return pl.pallas_call(
        _norm_gate_kernel,
        grid=(m // TM_NORM,),
        in_specs=[pl.BlockSpec((TM_NORM, d), lambda i: (i, 0)),
                  pl.BlockSpec((1, d), lambda i: (0, 0)),
                  pl.BlockSpec((None, GLA_GATE_RANK, d), lambda i: (layer, GA_OFF // GLA_GATE_RANK, 0)),
                  pl.BlockSpec((GATE_PAD, n), lambda i: (0, 0)),
                  pl.BlockSpec((1, n), lambda i: (0, 0))],
        out_specs=[pl.BlockSpec((TM_NORM, d), lambda i: (i, 0)),
                   pl.BlockSpec((TM_NORM, n), lambda i: (i, 0))],
        out_shape=[jax.ShapeDtypeStruct((m, d), BF16), jax.ShapeDtypeStruct((m, n), F32)],
        compiler_params=_cparams(("parallel",)),
        name="rmsnorm_gate",
    )(x, g.reshape(1, d), w_in_t, w2, b)


def _proj_wt_kernel(h_ref, w_ref, *rest, qk_norm):
    g_ref = rest[0] if qk_norm else None
    o_ref, wb_s = rest[-2:]

    @pl.when(pl.program_id(1) == 0)
    def _():
        wb_s[...] = w_ref[0].astype(BF16)

    h = h_ref[...]
    if not qk_norm:
        o_ref[...] = lax.dot_general(h, wb_s[...], NT_DIMS, preferred_element_type=F32).astype(o_ref.dtype)
        return
    for n in range(o_ref.shape[1] // MXU_N):
        acc = lax.dot_general(h, wb_s[n * MXU_N:(n + 1) * MXU_N, :], NT_DIMS, preferred_element_type=F32)
        for hh in range(MXU_N // DIL_DH):
            cs = slice(n * MXU_N + hh * DIL_DH, n * MXU_N + (hh + 1) * DIL_DH)
            a = acc[:, hh * DIL_DH:(hh + 1) * DIL_DH]
            ms = jnp.mean(a * a, axis=-1, keepdims=True)
            o_ref[:, cs] = (a * lax.rsqrt(ms + RMS_EPS) * g_ref[:, cs]).astype(o_ref.dtype)


def _proj_wt(h, w_in_t, layer, col0, n, gain=None, name="proj"):
    m, k = h.shape
    qk_norm = gain is not None
    in_specs = [pl.BlockSpec((TM_PROJ, k), lambda j, i: (i, 0)),
                pl.BlockSpec((pl.Element(1), pl.Element(TN_PROJ), pl.Element(k)),
                             lambda j, i: (layer, pl.multiple_of(col0 + j * TN_PROJ, F32_ROWS), 0))]
    operands = [h, w_in_t]
    if qk_norm:
        in_specs.append(pl.BlockSpec((1, TN_PROJ), lambda j, i: (0, j)))
        operands.append(gain)
    return pl.pallas_call(
        functools.partial(_proj_wt_kernel, qk_norm=qk_norm),
        grid=(n // TN_PROJ, m // TM_PROJ),
        in_specs=in_specs,
        out_specs=pl.BlockSpec((TM_PROJ, TN_PROJ), lambda j, i: (i, j)),
        out_shape=jax.ShapeDtypeStruct((m, n), BF16),
        scratch_shapes=[pltpu.VMEM((TN_PROJ, k), BF16)],
        compiler_params=_cparams(("parallel", "arbitrary")),
        name=name,
    )(*operands)


def _gla_kernel(q_ref, k_ref, v_ref, r_ref, lg_ref, gn_ref, o_ref,
                qt_s, kt_s, kd_s, dec_s, st_s, o_s):
    s_len, c, n_chunks = SEQ, GLA_CHUNK, SEQ // GLA_CHUNK
    g = lg_ref[...]
    g_hi = g.astype(BF16)
    rem = g - g_hi.astype(F32)
    g_mid = rem.astype(BF16)
    g_lo = (rem - g_mid.astype(F32)).astype(BF16)
    span = 2 * c
    ri = lax.broadcasted_iota(jnp.int32, (span, span), 0)
    ci = lax.broadcasted_iota(jnp.int32, (span, span), 1)
    tri = jnp.logical_and(ri >= ci, ri // c == ci // c).astype(BF16)
    b = jnp.concatenate(
        [sum(jnp.dot(tri, part[r0:r0 + span], preferred_element_type=F32) for part in (g_hi, g_mid, g_lo))
         for r0 in range(0, s_len, span)], axis=0)

    def chunk_last(a):
        a3 = a.reshape(n_chunks, c, GLA_DK)
        return jnp.broadcast_to(a3[:, c - 1:c, :], a3.shape).reshape(s_len, GLA_DK)

    eb = jnp.exp2(b)
    q = q_ref[...].astype(F32) * (GLA_DK ** -0.5)
    k = k_ref[...].astype(F32)
    qt_s[...] = (q * eb).astype(BF16)
    kt_s[...] = (k * jnp.exp2(-b)).astype(BF16)
    kd_s[...] = (k * jnp.exp2(chunk_last(b) - b)).astype(BF16)
    dec_s[...] = chunk_last(eb)
    st_s[...] = jnp.zeros_like(st_s)

    ri = lax.broadcasted_iota(jnp.int32, (c, c), 0)
    ci = lax.broadcasted_iota(jnp.int32, (c, c), 1)
    causal = ri >= ci

    nt_dims = (((1,), (1,)), ((), ()))
    tn_dims = (((0,), (0,)), ((), ()))

    def chunk_group(gi, carry):
        starts = [pl.multiple_of((gi * GLA_GROUP + j) * c, c) for j in range(GLA_GROUP)]
        rows = [pl.ds(r0, c) for r0 in starts]
        qts = [qt_s[r, :] for r in rows]
        vs = [v_ref[r, :] for r in rows]
        attn = [lax.dot_general(qt, kt_s[r, :], nt_dims, preferred_element_type=F32)
                for qt, r in zip(qts, rows)]
        cs_t = [lax.dot_general(v, kd_s[r, :], tn_dims, preferred_element_type=F32)
                for v, r in zip(vs, rows)]
        attn = [jnp.where(causal, a, 0.0).astype(BF16) for a in attn]
        o_intra = [jnp.dot(a, v, preferred_element_type=F32) for a, v in zip(attn, vs)]
        st = st_s[...]
        states = []
        for r0, cs in zip(starts, cs_t):
            states.append(st.astype(BF16))
            st = st * dec_s[pl.ds(r0, F32_ROWS), :][0:1, :] + cs
        st_s[...] = st
        o_inter = [lax.dot_general(qt, sb, nt_dims, preferred_element_type=F32)
                   for qt, sb in zip(qts, states)]
        for r, a, b_ in zip(rows, o_intra, o_inter):
            o_s[r, :] = a + b_
        return carry

    lax.fori_loop(0, n_chunks // GLA_GROUP, chunk_group, 0)

    o = o_s[...]
    ms = jnp.mean(o * o, axis=-1, keepdims=True)
    y = o * lax.rsqrt(ms + RMS_EPS) * gn_ref[...]
    r = r_ref[...].astype(F32)
    o_ref[...] = (y * (r * jax.nn.sigmoid(r))).astype(o_ref.dtype)


def _gla(pa, log_g, onorm_g):
    s = SEQ
    return pl.pallas_call(
        _gla_kernel,
        grid=(BATCH, GLA_HEADS),
        in_specs=[
            pl.BlockSpec((s, GLA_DK), lambda b, h: (b, PA_QA // GLA_DK + h)),
            pl.BlockSpec((s, GLA_DK), lambda b, h: (b, PA_KA // GLA_DK + h)),
            pl.BlockSpec((s, GLA_DV), lambda b, h: (b, PA_VA // GLA_DV + h)),
            pl.BlockSpec((s, GLA_DV), lambda b, h: (b, PA_RA // GLA_DV + h)),
            pl.BlockSpec((s, GLA_DK), lambda b, h: (b, h)),
            pl.BlockSpec((1, GLA_DV), lambda b, h: (0, 0)),
        ],
        out_specs=pl.BlockSpec((s, GLA_DV), lambda b, h: (b, h)),
        out_shape=jax.ShapeDtypeStruct((M_TOK, VA_W), BF16),
        scratch_shapes=[
            pltpu.VMEM((s, GLA_DK), BF16),
            pltpu.VMEM((s, GLA_DK), BF16),
            pltpu.VMEM((s, GLA_DK), BF16),
            pltpu.VMEM((s, GLA_DK), F32),
            pltpu.VMEM((GLA_DV, GLA_DK), F32),
            pltpu.VMEM((s, GLA_DV), F32),
        ],
        compiler_params=_cparams(("parallel", "parallel")),
        name="gla_mixer",
    )(pa, pa, pa, pa, log_g, onorm_g.reshape(1, GLA_DV))


def _t5_bucket(dist):
    max_exact = REL_BUCKETS // 2
    safe = np.maximum(dist, 1)
    large = max_exact + (np.log(safe / max_exact) / np.log(REL_MAX_DIST / max_exact)
                         * (REL_BUCKETS - max_exact)).astype(np.int64)
    large = np.minimum(large, REL_BUCKETS - 1)
    return np.where(dist < max_exact, dist, large).astype(np.int32)


ATT_B3, ATT_B2_FIRST, ATT_B2_REST, ATT_B1_FIRST, ATT_B1_REST = range(5)
B2_SLABS, B2_Q_ROWS = 4, 32
B1_SLABS, B1_Q_ROWS = 16, 8
MASKED_BUCKET = -1


def _attention_tables():
    far = 10 ** 6

    def slab_tokens(n_slabs, rows, first_pos, class_step):
        pos = first_pos + np.arange(rows)
        return (N_CLASS * pos[None, :] + class_step * np.arange(n_slabs)[:, None]).reshape(-1)

    specs = []
    a = np.arange(CLASS_LEN)
    specs.append((N_CLASS * a, np.concatenate([N_CLASS * a, np.full(CLASS_LEN, far)]), DIL_PATTERNS[2][0]))
    for first in (True, False):
        tq = slab_tokens(B2_SLABS, B2_Q_ROWS, 0, 4)
        tk = slab_tokens(B2_SLABS, 2 * B2_Q_ROWS, 0 if first else -B2_Q_ROWS, 4)
        specs.append((tq, tk, DIL_PATTERNS[1][0]))
    for first in (True, False):
        tq = slab_tokens(B1_SLABS, B1_Q_ROWS, 0, 1)
        tk = slab_tokens(B1_SLABS, 2 * B1_Q_ROWS, 0 if first else -B1_Q_ROWS, 1)
        specs.append((tq, tk, DIL_PATTERNS[0][0]))
    buckets = []
    for tq, tk, window in specs:
        dist = tq[:, None] - tk[None, :]
        in_band = (dist >= 0) & (dist <= window)
        buckets.append(np.where(in_band, _t5_bucket(np.clip(dist, 0, None)), MASKED_BUCKET))
    return np.stack(buckets).astype(np.int32)


def _bias_kernel(rb_ref, bk_ref, o_ref):
    bk = bk_ref[...]
    for h in range(DIL_HEADS):
        acc = jnp.full(bk.shape, -jnp.inf, F32)
        for bucket in range(REL_BUCKETS):
            acc = jnp.where(bk == bucket, rb_ref[bucket, h] * LOG2E, acc)
        o_ref[h] = acc


def _bias_tables(rel_bias, buckets):
    nt, q, q2 = buckets.shape
    return pl.pallas_call(
        _bias_kernel,
        grid=(nt,),
        in_specs=[pl.BlockSpec(memory_space=pltpu.SMEM),
                  pl.BlockSpec((None, q, q2), lambda i: (i, 0, 0))],
        out_specs=pl.BlockSpec((DIL_HEADS, None, q, q2), lambda i: (0, i, 0, 0)),
        out_shape=jax.ShapeDtypeStruct((DIL_HEADS, nt, q, q2), F32),
        compiler_params=_cparams(("parallel",)),
        name="t5_bias_tables",
    )(rel_bias, buckets)


def _gather(ref, slabs):
    return jnp.concatenate([ref[pl.ds(s, n), :] for s, n in slabs], axis=0)


def _scatter(ref, slabs, val):
    off = 0
    for s, n in slabs:
        ref[pl.ds(s, n), :] = val[off:off + n]
        off += n


def _attn_kernel(q_ref, k_ref, v_ref, bias_ref, o_ref,
                 q32_s, k32_s, v32_s, acc_s, m_s, l_s):
    qb = DIL_BLOCK
    q32_s[...] = q_ref[...].astype(F32)
    k32_s[...] = k_ref[...].astype(F32)
    v32_s[...] = v_ref[...].astype(F32)

    def b3_block(u):
        slab = [(u * CLASS_LEN, CLASS_LEN)]
        return (q_ref, k_ref, v_ref, ATT_B3, CLASS_LEN, slab, slab)

    def b2_block(e, kk):
        q0 = kk * B2_Q_ROWS
        k0 = max(q0 - B2_Q_ROWS, 0)
        bases = [(DIL_PATTERNS[1][1] * c + e) * CLASS_LEN for c in range(B2_SLABS)]
        return (q_ref, k_ref, v_ref, ATT_B2_FIRST if kk == 0 else ATT_B2_REST, 2 * qb,
                [(base + q0, B2_Q_ROWS) for base in bases], [(base + k0, 2 * B2_Q_ROWS) for base in bases])

    def b1_block(kk):
        q0 = kk * B1_Q_ROWS
        k0 = max(q0 - B1_Q_ROWS, 0)
        return (q32_s, k32_s, v32_s, ATT_B1_FIRST if kk == 0 else ATT_B1_REST, 2 * qb,
                [(u * CLASS_LEN + q0, B1_Q_ROWS) for u in range(B1_SLABS)],
                [(u * CLASS_LEN + k0, 2 * B1_Q_ROWS) for u in range(B1_SLABS)])

    def logits_of(group):
        return [lax.dot_general(_gather(qs, q_sl).astype(BF16), _gather(ks, k_sl).astype(BF16),
                                NT_DIMS, preferred_element_type=F32)
                for qs, ks, _, _, _, q_sl, k_sl in group]

    def update(group, logits, init):
        olds = [None if init else (_gather(m_s, q_sl), _gather(l_s, q_sl), _gather(acc_s, q_sl))
                for *_, q_sl, _ in group]
        probs, stats = [], []
        for s, (_, _, _, tab, nk, _, _), old in zip(logits, group, olds):
            s = s + bias_ref[tab, :, 0:nk]
            m_blk = jnp.broadcast_to(jnp.max(s, axis=-1, keepdims=True), (qb, LANE))
            if init:
                m_new, alpha = m_blk, None
            else:
                m_new = jnp.maximum(old[0], m_blk)
                alpha = jnp.exp2(old[0] - m_new)
            p = jnp.exp2(s - jnp.concatenate([m_new] * (nk // LANE), axis=1))
            probs.append(p.astype(BF16))
            stats.append((m_new, alpha))
        pvs = [jnp.dot(p, jnp.concatenate([_gather(vs, k_sl).astype(BF16), jnp.ones((nk, LANE), BF16)], axis=1),
                       preferred_element_type=F32)
               for p, (_, _, vs, _, nk, _, k_sl) in zip(probs, group)]
        for pvl, (m_new, alpha), old, (*_, q_sl, _) in zip(pvs, stats, olds, group):
            pv, l_blk = pvl[:, :DIL_DH], pvl[:, DIL_DH:]
            _scatter(m_s, q_sl, m_new)
            _scatter(l_s, q_sl, l_blk if init else alpha * old[1] + l_blk)
            _scatter(acc_s, q_sl, pv if init else alpha * old[2] + pv)

    n_res = DIL_PATTERNS[1][1]
    blocks = ([(b3_block(u), True) for u in range(N_CLASS)]
              + [(b2_block(e, kk), False) for kk in range(CLASS_LEN // B2_Q_ROWS) for e in range(n_res)]
              + [(b1_block(kk), False) for kk in range(CLASS_LEN // B1_Q_ROWS)])
    assert N_CLASS % ATT_GROUP == 0 and len(blocks) % ATT_GROUP == 0
    groups = [blocks[g:g + ATT_GROUP] for g in range(0, len(blocks), ATT_GROUP)]
    for grp in groups:
        blks, init = [b for b, _ in grp], grp[0][1]
        update(blks, logits_of(blks), init)

    o_ref[...] = (acc_s[...] / l_s[...]).astype(o_ref.dtype)


def _attention(qk_p, v_p, bias_tabs):
    s = SEQ
    nt = bias_tabs.shape[1]
    heads = DIL_HEADS
    return pl.pallas_call(
        _attn_kernel,
        grid=(BATCH, heads),
        in_specs=[
            pl.BlockSpec((s, DIL_DH), lambda b, h: (b, h)),
            pl.BlockSpec((s, DIL_DH), lambda b, h: (b, heads + h)),
            pl.BlockSpec((s, DIL_DH), lambda b, h: (b, h)),
            pl.BlockSpec((None, nt, DIL_BLOCK, 2 * DIL_BLOCK), lambda b, h: (h, 0, 0, 0)),
        ],
        out_specs=pl.BlockSpec((s, DIL_DH), lambda b, h: (b, h)),
        out_shape=jax.ShapeDtypeStruct((M_TOK, DIL_W), BF16),
        scratch_shapes=[
            pltpu.VMEM((s, DIL_DH), F32),
            pltpu.VMEM((s, DIL_DH), F32),
            pltpu.VMEM((s, DIL_DH), F32),
            pltpu.VMEM((s, DIL_DH), F32),
            pltpu.VMEM((s, LANE), F32),
            pltpu.VMEM((s, LANE), F32),
        ],
        compiler_params=_cparams(("parallel", "parallel")),
        name="dilated_attention",
    )(qk_p, qk_p, v_p, bias_tabs)


def _to_class_order(a):
    w = a.shape[1]
    return a.reshape(BATCH, CLASS_LEN, N_CLASS, w).transpose(0, 2, 1, 3).reshape(M_TOK, w)


def _to_token_order(a):
    w = a.shape[1]
    return a.reshape(BATCH, N_CLASS, CLASS_LEN, w).transpose(0, 2, 1, 3).reshape(M_TOK, w)


def _outproj_kernel(oa_ref, ob_ref, x_ref, w_ref, g_ref, x1_ref, h2_ref, wb_s):
    @pl.when(pl.program_id(0) == 0)
    def _():
        wb_s[...] = w_ref[...].astype(BF16)

    oa, ob = oa_ref[...], ob_ref[...]
    d = x_ref.shape[1]
    ssq = jnp.zeros((x_ref.shape[0], 1), F32)
    for n in range(d // OUT_CHUNK):
        cs = slice(n * OUT_CHUNK, (n + 1) * OUT_CHUNK)
        acc = jnp.dot(oa, wb_s[0:VA_W, cs], preferred_element_type=F32)
        acc = acc + jnp.dot(ob, wb_s[VA_W:, cs], preferred_element_type=F32)
        x1 = x_ref[:, cs] + acc
        x1_ref[:, cs] = x1
        ssq = ssq + jnp.sum(x1 * x1, axis=-1, keepdims=True)
    scale = lax.rsqrt(ssq * (1.0 / d) + RMS_EPS)
    h2_ref[...] = (x1_ref[...] * scale * g_ref[...]).astype(h2_ref.dtype)


def _outproj(out_a, out_b, x, w_out, layer, norm2_g):
    m, d = x.shape
    tm = TM_OUT
    row = lambda w: pl.BlockSpec((tm, w), lambda i: (i, 0))
    return pl.pallas_call(
        _outproj_kernel,
        grid=(m // tm,),
        in_specs=[row(VA_W), row(DIL_W), row(d),
                  pl.BlockSpec((None, d, d), lambda i: (layer, 0, 0), pipeline_mode=pl.Buffered(1)),
                  pl.BlockSpec((1, d), lambda i: (0, 0))],
        out_specs=[row(d), row(d)],
        out_shape=[jax.ShapeDtypeStruct((m, d), F32), jax.ShapeDtypeStruct((m, d), BF16)],
        scratch_shapes=[pltpu.VMEM((d, d), BF16)],
        compiler_params=_cparams(("arbitrary",)),
        name="outproj_norm2",
    )(out_a, out_b, x, w_out, norm2_g.reshape(1, d))


def _ffn_kernel(h_ref, x_hbm, wg_ref, wu_ref, wd_ref, o_ref, x_sem):
    i, j = pl.program_id(0), pl.program_id(1)
    tm = o_ref.shape[0]

    def hidden():
        h = h_ref[...]
        gate = jnp.dot(h, wg_ref[...].astype(BF16), preferred_element_type=F32)
        up = jnp.dot(h, wu_ref[...].astype(BF16), preferred_element_type=F32)
        return (gate * jax.nn.sigmoid(gate) * up).astype(BF16)

    def down(act):
        return jnp.dot(act, wd_ref[...].astype(BF16), preferred_element_type=F32)

    @pl.when(j == 0)
    def _():
        rows = pl.ds(pl.multiple_of(i * tm, tm), tm)
        residual = pltpu.make_async_copy(x_hbm.at[rows, :], o_ref, x_sem)
        residual.start()
        act = hidden()
        residual.wait()
        o_ref[...] += down(act)

    @pl.when(j > 0)
    def _():
        o_ref[...] += down(hidden())


def _ffn(h2, x1, w_gate, w_up, w_down, layer):
    m, d = x1.shape
    f = w_gate.shape[2]
    tm, tf = TM_FFN, TF_FFN
    return pl.pallas_call(
        _ffn_kernel,
        grid=(m // tm, f // tf),
        in_specs=[pl.BlockSpec((tm, d), lambda i, j: (i, 0)),
                  pl.BlockSpec(memory_space=pl.ANY),
                  pl.BlockSpec((None, d, tf), lambda i, j: (layer, 0, j)),
                  pl.BlockSpec((None, d, tf), lambda i, j: (layer, 0, j)),
                  pl.BlockSpec((None, tf, d), lambda i, j: (layer, j, 0))],
        out_specs=pl.BlockSpec((tm, d), lambda i, j: (i, 0)),
        out_shape=jax.ShapeDtypeStruct((m, d), F32),
        scratch_shapes=[pltpu.SemaphoreType.DMA(())],
        compiler_params=_cparams(("parallel", "arbitrary")),
        name="swiglu_ffn",
    )(h2, x1, w_gate, w_up, w_down)


def kernel(x, norm1_g, w_in, gla_gate_w2, gla_gate_b, gla_onorm_g, q_norm_g, k_norm_g, rel_bias,
           w_out, norm2_g, w_gate, w_up, w_down):
    bsz, s, d = x.shape
    assert (bsz, s, d) == (BATCH, SEQ, D_MODEL)
    bias_tabs = _bias_tables(rel_bias.astype(F32), jnp.asarray(_attention_tables()))
    w_in_t = jnp.swapaxes(w_in, 1, 2)

    xf = x.reshape(M_TOK, D_MODEL)
    for l in range(DEPTH):
        w_gate2 = jnp.pad(gla_gate_w2[l], ((0, GATE_PAD - GLA_GATE_RANK), (0, 0))).astype(BF16)
        qk_gain = jnp.concatenate([jnp.tile(q_norm_g[l], DIL_HEADS) * (DIL_DH ** -0.5 * LOG2E),
                                   jnp.tile(k_norm_g[l], DIL_HEADS)]).reshape(1, 2 * DIL_W).astype(F32)

        h, log_g = _norm_gate(xf, norm1_g[l], w_in_t, l, w_gate2,
                              gla_gate_b[l].reshape(1, QA_W).astype(F32))
        h_p = _to_class_order(h)
        pa = _proj_wt(h, w_in_t, l, 0, PA_W, name="proj_gla")
        h_p, pa = lax.optimization_barrier((h_p, pa))
        qk_p = _proj_wt(h_p, w_in_t, l, PB_OFF, 2 * DIL_W, gain=qk_gain, name="proj_qknorm")
        v_p = _proj_wt(h_p, w_in_t, l, PB_OFF + 2 * DIL_W, DIL_W, name="proj_v")
        out_b_p = _attention(qk_p, v_p, bias_tabs)
        out_b_p, pa = lax.optimization_barrier((out_b_p, pa))
        out_b = _to_token_order(out_b_p)
        out_a = _gla(pa, log_g, gla_onorm_g[l])

        x1, h2 = _outproj(out_a, out_b, xf, w_out, l, norm2_g[l])
        xf = _ffn(h2, x1, w_gate, w_up, w_down, l)
    return xf.reshape(bsz, s, d)
```

```python
import functools
import math

import numpy as np
import jax
import jax.numpy as jnp
from jax import lax
from jax.experimental import pallas as pl
from jax.experimental.pallas import tpu as pltpu

D_MODEL = 2048
BATCH = 4
SEQ = 2048
DEPTH = 2
M_TOK = BATCH * SEQ

GLA_HEADS = 4
GLA_DV = 256
GLA_DK = 128
GLA_GATE_RANK = 16
GLA_GATE_TAU = 16.0
GLA_CHUNK = 64
DIL_HEADS = 8
DIL_DH = 128
DIL_PATTERNS = ((128, 1), (512, 4), (2048, 16))
DIL_BLOCK = 128
REL_BUCKETS = 32
REL_MAX_DIST = 2048
FFN_HIDDEN = 5632
RMS_EPS = 1e-6

QA_W = GLA_HEADS * GLA_DK
VA_W = GLA_HEADS * GLA_DV
DIL_W = DIL_HEADS * DIL_DH
PA_QA, PA_KA, PA_VA, PA_RA = 0, QA_W, 2 * QA_W, 2 * QA_W + VA_W
PA_W = PA_RA + VA_W
GA_OFF = PA_W
PB_OFF = GA_OFF + GLA_GATE_RANK
PB_W = 3 * DIL_W
LANE = 128
MXU_N = 256
F32_ROWS = 8
GATE_PAD = LANE

N_CLASS = 16
CLASS_LEN = SEQ // N_CLASS

BF16 = jnp.bfloat16
F32 = jnp.float32

VMEM_LIMIT = 56 * 1024 * 1024

TM_NORM = 512
TM_PROJ, TN_PROJ = 1024, 1024
TM_OUT, OUT_CHUNK = 512, 512
TM_FFN, TF_FFN = 1024, 512
GLA_GROUP = 8
ATT_GROUP = 16
LOG2E = math.log2(math.e)


def _cparams(sem):
    return pltpu.CompilerParams(dimension_semantics=sem, vmem_limit_bytes=VMEM_LIMIT)


NT_DIMS = (((1,), (1,)), ((), ()))


def _norm_gate_kernel(x_ref, g_ref, w1_ref, w2_ref, b_ref, h_ref, lg_ref):
    x = x_ref[...]
    ms = jnp.mean(x * x, axis=-1, keepdims=True)
    h = (x * lax.rsqrt(ms + RMS_EPS) * g_ref[...]).astype(BF16)
    h_ref[...] = h
    w1 = w1_ref[...].astype(BF16)
    w1 = jnp.concatenate([w1, jnp.zeros((GATE_PAD - w1.shape[0], w1.shape[1]), BF16)], axis=0)
    ga = lax.dot_general(h, w1, NT_DIMS, preferred_element_type=F32)
    pre = jnp.dot(ga.astype(BF16), w2_ref[...], preferred_element_type=F32) + b_ref[...]
    y = pre * LOG2E
    lg_ref[...] = (jnp.minimum(y, 0.0) - jnp.log2(1.0 + jnp.exp2(-jnp.abs(y)))) * (1.0 / GLA_GATE_TAU)


def _norm_gate(x, g, w_in_t, layer, w2, b):
    m, d = x.shape
    n = w2.shape[1]
    assert GA_OFF % GLA_GATE_RANK == 0
    return pl.pallas_call(
        _norm_gate_kernel,
        grid=(m // TM_NORM,),
        in_specs=[pl.BlockSpec((TM_NORM, d), lambda i: (i, 0)),
                  pl.BlockSpec((1, d), lambda i: (0, 0)),
                  pl.BlockSpec((None, GLA_GATE_RANK, d), lambda i: (layer, GA_OFF // GLA_GATE_RANK, 0)),
                  pl.BlockSpec((GATE_PAD, n), lambda i: (0, 0)),
                  pl.BlockSpec((1, n), lambda i: (0, 0))],
        out_specs=[pl.BlockSpec((TM_NORM, d), lambda i: (i, 0)),
                   pl.BlockSpec((TM_NORM, n), lambda i: (i, 0))],
        out_shape=[jax.ShapeDtypeStruct((m, d), BF16), jax.ShapeDtypeStruct((m, n), F32)],
        compiler_params=_cparams(("parallel",)),
        name="rmsnorm_gate",
    )(x, g.reshape(1, d), w_in_t, w2, b)


def _proj_wt_kernel(h_ref, w_ref, *rest, qk_norm):
    g_ref = rest[0] if qk_norm else None
    o_ref, wb_s = rest[-2:]

    @pl.when(pl.program_id(1) == 0)
    def _():
        wb_s[...] = w_ref[0].astype(BF16)

    h = h_ref[...]
    if not qk_norm:
        o_ref[...] = lax.dot_general(h, wb_s[...], NT_DIMS, preferred_element_type=F32).astype(o_ref.dtype)
        return
    for n in range(o_ref.shape[1] // MXU_N):
        acc = lax.dot_general(h, wb_s[n * MXU_N:(n + 1) * MXU_N, :], NT_DIMS, preferred_element_type=F32)
        for hh in range(MXU_N // DIL_DH):
            cs = slice(n * MXU_N + hh * DIL_DH, n * MXU_N + (hh + 1) * DIL_DH)
            a = acc[:, hh * DIL_DH:(hh + 1) * DIL_DH]
            ms = jnp.mean(a * a, axis=-1, keepdims=True)
            o_ref[:, cs] = (a * lax.rsqrt(ms + RMS_EPS) * g_ref[:, cs]).astype(o_ref.dtype)


def _proj_wt(h, w_in_t, layer, col0, n, gain=None, name="proj"):
    m, k = h.shape
    qk_norm = gain is not None
    in_specs = [pl.BlockSpec((TM_PROJ, k), lambda j, i: (i, 0)),
                pl.BlockSpec((pl.Element(1), pl.Element(TN_PROJ), pl.Element(k)),
                             lambda j, i: (layer, pl.multiple_of(col0 + j * TN_PROJ, F32_ROWS), 0))]
    operands = [h, w_in_t]
    if qk_norm:
        in_specs.append(pl.BlockSpec((1, TN_PROJ), lambda j, i: (0, j)))
        operands.append(gain)
    return pl.pallas_call(
        functools.partial(_proj_wt_kernel, qk_norm=qk_norm),
        grid=(n // TN_PROJ, m // TM_PROJ),
        in_specs=in_specs,
        out_specs=pl.BlockSpec((TM_PROJ, TN_PROJ), lambda j, i: (i, j)),
        out_shape=jax.ShapeDtypeStruct((m, n), BF16),
        scratch_shapes=[pltpu.VMEM((TN_PROJ, k), BF16)],
        compiler_params=_cparams(("parallel", "arbitrary")),
        name=name,
    )(*operands)


def _gla_kernel(q_ref, k_ref, v_ref, r_ref, lg_ref, gn_ref, o_ref,
                qt_s, kt_s, kd_s, dec_s, st_s, o_s):
    c = GLA_CHUNK
    part_rows = GLA_GROUP * c
    tn_dims = (((0,), (0,)), ((), ()))
    ri = lax.broadcasted_iota(jnp.int32, (c, c), 0)
    ci = lax.broadcasted_iota(jnp.int32, (c, c), 1)
    causal = ri >= ci
    row_in_chunk = lax.broadcasted_iota(jnp.int32, (part_rows, GLA_DK), 0) & (c - 1)

    def chunk_last(a):
        a3 = a.reshape(GLA_GROUP, c, GLA_DK)
        return jnp.broadcast_to(a3[:, c - 1:c, :], a3.shape).reshape(part_rows, GLA_DK)

    def prologue(rows):
        b = lg_ref[rows, :]
        shift = 1
        while shift < c:
            b = b + jnp.where(row_in_chunk >= shift, pltpu.roll(b, shift, axis=0), 0.0)
            shift *= 2
        eb = jnp.exp2(b)
        q = q_ref[rows, :].astype(F32) * (GLA_DK ** -0.5)
        k = k_ref[rows, :].astype(F32)
        qt_s[rows, :] = (q * eb).astype(BF16)
        kt_s[rows, :] = (k * jnp.exp2(-b)).astype(BF16)
        kd_s[rows, :] = (k * jnp.exp2(chunk_last(b) - b)).astype(BF16)
        dec_s[rows, :] = chunk_last(eb)

    def chunks(first_row):
        starts = [first_row + j * c for j in range(GLA_GROUP)]
        rows = [slice(r0, r0 + c) for r0 in starts]
        qts = [qt_s[r, :] for r in rows]
        vs = [v_ref[r, :] for r in rows]
        attn = [lax.dot_general(qt, kt_s[r, :], NT_DIMS, preferred_element_type=F32)
                for qt, r in zip(qts, rows)]
        cs_t = [lax.dot_general(v, kd_s[r, :], tn_dims, preferred_element_type=F32)
                for v, r in zip(vs, rows)]
        attn = [jnp.where(causal, a, 0.0).astype(BF16) for a in attn]
        o_intra = [jnp.dot(a, v, preferred_element_type=F32) for a, v in zip(attn, vs)]
        st = st_s[...]
        states = []
        for r0, cs in zip(starts, cs_t):
            states.append(st.astype(BF16))
            st = st * dec_s[r0:r0 + 1, :] + cs
        st_s[...] = st
        o_inter = [lax.dot_general(qt, sb, NT_DIMS, preferred_element_type=F32)
                   for qt, sb in zip(qts, states)]
        for r, a, b_ in zip(rows, o_intra, o_inter):
            o_s[r, :] = a + b_

    def epilogue(rows):
        o = o_s[rows, :]
        ms = jnp.mean(o * o, axis=-1, keepdims=True)
        y = o * lax.rsqrt(ms + RMS_EPS) * gn_ref[...]
        r = r_ref[rows, :].astype(F32)
        o_ref[rows, :] = (y * (r * jax.nn.sigmoid(r))).astype(o_ref.dtype)

    st_s[...] = jnp.zeros_like(st_s)
    for first_row in range(0, SEQ, part_rows):
        rows = slice(first_row, first_row + part_rows)
        prologue(rows)
        chunks(first_row)
        epilogue(rows)


def _gla(pa, log_g, onorm_g):
    s = SEQ
    return pl.pallas_call(
        _gla_kernel,
        grid=(BATCH, GLA_HEADS),
        in_specs=[
            pl.BlockSpec((s, GLA_DK), lambda b, h: (b, PA_QA // GLA_DK + h)),
            pl.BlockSpec((s, GLA_DK), lambda b, h: (b, PA_KA // GLA_DK + h)),
            pl.BlockSpec((s, GLA_DV), lambda b, h: (b, PA_VA // GLA_DV + h)),
            pl.BlockSpec((s, GLA_DV), lambda b, h: (b, PA_RA // GLA_DV + h)),
            pl.BlockSpec((s, GLA_DK), lambda b, h: (b, h)),
            pl.BlockSpec((1, GLA_DV), lambda b, h: (0, 0)),
        ],
        out_specs=pl.BlockSpec((s, GLA_DV), lambda b, h: (b, h)),
        out_shape=jax.ShapeDtypeStruct((M_TOK, VA_W), BF16),
        scratch_shapes=[
            pltpu.VMEM((s, GLA_DK), BF16),
            pltpu.VMEM((s, GLA_DK), BF16),
            pltpu.VMEM((s, GLA_DK), BF16),
            pltpu.VMEM((s, GLA_DK), F32),
            pltpu.VMEM((GLA_DV, GLA_DK), F32),
            pltpu.VMEM((s, GLA_DV), F32),
        ],
        compiler_params=_cparams(("parallel", "parallel")),
        name="gla_mixer",
    )(pa, pa, pa, pa, log_g, onorm_g.reshape(1, GLA_DV))


def _t5_bucket(dist):
    max_exact = REL_BUCKETS // 2
    safe = np.maximum(dist, 1)
    large = max_exact + (np.log(safe / max_exact) / np.log(REL_MAX_DIST / max_exact)
                         * (REL_BUCKETS - max_exact)).astype(np.int64)
    large = np.minimum(large, REL_BUCKETS - 1)
    return np.where(dist < max_exact, dist, large).astype(np.int32)


ATT_B3, ATT_B2_FIRST, ATT_B2_REST, ATT_B1_FIRST, ATT_B1_REST = range(5)
B2_SLABS, B2_Q_ROWS = 4, 32
B1_SLABS, B1_Q_ROWS = 16, 8
MASKED_BUCKET = -1


def _attention_tables():
    far = 10 ** 6

    def slab_tokens(n_slabs, rows, first_pos, class_step):
        pos = first_pos + np.arange(rows)
        return (N_CLASS * pos[None, :] + class_step * np.arange(n_slabs)[:, None]).reshape(-1)

    specs = []
    a = np.arange(CLASS_LEN)
    specs.append((N_CLASS * a, np.concatenate([N_CLASS * a, np.full(CLASS_LEN, far)]), DIL_PATTERNS[2][0]))
    for first in (True, False):
        tq = slab_tokens(B2_SLABS, B2_Q_ROWS, 0, 4)
        tk = slab_tokens(B2_SLABS, 2 * B2_Q_ROWS, 0 if first else -B2_Q_ROWS, 4)
        specs.append((tq, tk, DIL_PATTERNS[1][0]))
    for first in (True, False):
        tq = slab_tokens(B1_SLABS, B1_Q_ROWS, 0, 1)
        tk = slab_tokens(B1_SLABS, 2 * B1_Q_ROWS, 0 if first else -B1_Q_ROWS, 1)
        specs.append((tq, tk, DIL_PATTERNS[0][0]))
    buckets = []
    for tq, tk, window in specs:
        dist = tq[:, None] - tk[None, :]
        in_band = (dist >= 0) & (dist <= window)
        buckets.append(np.where(in_band, _t5_bucket(np.clip(dist, 0, None)), MASKED_BUCKET))
    return np.stack(buckets).astype(np.int32)


def _bias_kernel(rb_ref, bk_ref, o_ref):
    bk = bk_ref[...]
    for h in range(DIL_HEADS):
        acc = jnp.full(bk.shape, -jnp.inf, F32)
        for bucket in range(REL_BUCKETS):
            acc = jnp.where(bk == bucket, rb_ref[bucket, h] * LOG2E, acc)
        o_ref[h] = acc


def _bias_tables(rel_bias, buckets):
    nt, q, q2 = buckets.shape
    return pl.pallas_call(
        _bias_kernel,
        grid=(nt,),
        in_specs=[pl.BlockSpec(memory_space=pltpu.SMEM),
                  pl.BlockSpec((None, q, q2), lambda i: (i, 0, 0))],
        out_specs=pl.BlockSpec((DIL_HEADS, None, q, q2), lambda i: (0, i, 0, 0)),
        out_shape=jax.ShapeDtypeStruct((DIL_HEADS, nt, q, q2), F32),
        compiler_params=_cparams(("parallel",)),
        name="t5_bias_tables",
    )(rel_bias, buckets)


def _gather(ref, slabs):
    return jnp.concatenate([ref[pl.ds(s, n), :] for s, n in slabs], axis=0)


def _scatter(ref, slabs, val):
    off = 0
    for s, n in slabs:
        ref[pl.ds(s, n), :] = val[off:off + n]
        off += n


def _attn_kernel(q_ref, k_ref, v_ref, bias_ref, o_ref,
                 q32_s, k32_s, v32_s, acc_s, m_s, l_s):
    qb = DIL_BLOCK
    q32_s[...] = q_ref[...].astype(F32)
    k32_s[...] = k_ref[...].astype(F32)
    v32_s[...] = v_ref[...].astype(F32)

    def b3_block(u):
        slab = [(u * CLASS_LEN, CLASS_LEN)]
        return (q_ref, k_ref, v_ref, ATT_B3, CLASS_LEN, slab, slab)

    def b2_block(e, kk):
        q0 = kk * B2_Q_ROWS
        k0 = max(q0 - B2_Q_ROWS, 0)
        bases = [(DIL_PATTERNS[1][1] * c + e) * CLASS_LEN for c in range(B2_SLABS)]
        return (q_ref, k_ref, v_ref, ATT_B2_FIRST if kk == 0 else ATT_B2_REST, 2 * qb,
                [(base + q0, B2_Q_ROWS) for base in bases], [(base + k0, 2 * B2_Q_ROWS) for base in bases])

    def b1_block(kk):
        q0 = kk * B1_Q_ROWS
        k0 = max(q0 - B1_Q_ROWS, 0)
        return (q32_s, k32_s, v32_s, ATT_B1_FIRST if kk == 0 else ATT_B1_REST, 2 * qb,
                [(u * CLASS_LEN + q0, B1_Q_ROWS) for u in range(B1_SLABS)],
                [(u * CLASS_LEN + k0, 2 * B1_Q_ROWS) for u in range(B1_SLABS)])

    def logits_of(group):
        return [lax.dot_general(_gather(qs, q_sl).astype(BF16), _gather(ks, k_sl).astype(BF16),
                                NT_DIMS, preferred_element_type=F32)
                for qs, ks, _, _, _, q_sl, k_sl in group]

    def update(group, logits, init):
        olds = [None if init else (_gather(m_s, q_sl), _gather(l_s, q_sl), _gather(acc_s, q_sl))
                for *_, q_sl, _ in group]
        probs, stats = [], []
        for s, (_, _, _, tab, nk, _, _), old in zip(logits, group, olds):
            s = s + bias_ref[tab, :, 0:nk]
            m_blk = jnp.broadcast_to(jnp.max(s, axis=-1, keepdims=True), (qb, LANE))
            if init:
                m_new, alpha = m_blk, None
            else:
                m_new = jnp.maximum(old[0], m_blk)
                alpha = jnp.exp2(old[0] - m_new)
            p = jnp.exp2(s - jnp.concatenate([m_new] * (nk // LANE), axis=1))
            probs.append(p.astype(BF16))
            stats.append((m_new, alpha))
        pvs = [jnp.dot(p, jnp.concatenate([_gather(vs, k_sl).astype(BF16), jnp.ones((nk, LANE), BF16)], axis=1),
                       preferred_element_type=F32)
               for p, (_, _, vs, _, nk, _, k_sl) in zip(probs, group)]
        for pvl, (m_new, alpha), old, (*_, q_sl, _) in zip(pvs, stats, olds, group):
            pv, l_blk = pvl[:, :DIL_DH], pvl[:, DIL_DH:]
            _scatter(m_s, q_sl, m_new)
            _scatter(l_s, q_sl, l_blk if init else alpha * old[1] + l_blk)
            _scatter(acc_s, q_sl, pv if init else alpha * old[2] + pv)

    n_res = DIL_PATTERNS[1][1]
    blocks = ([(b3_block(u), True) for u in range(N_CLASS)]
              + [(b2_block(e, kk), False) for kk in range(CLASS_LEN // B2_Q_ROWS) for e in range(n_res)]
              + [(b1_block(kk), False) for kk in range(CLASS_LEN // B1_Q_ROWS)])
    assert N_CLASS % ATT_GROUP == 0 and len(blocks) % ATT_GROUP == 0
    groups = [blocks[g:g + ATT_GROUP] for g in range(0, len(blocks), ATT_GROUP)]
    for grp in groups:
        blks, init = [b for b, _ in grp], grp[0][1]
        update(blks, logits_of(blks), init)

    o_ref[...] = (acc_s[...] / l_s[...]).astype(o_ref.dtype)


def _attention(qk_p, v_p, bias_tabs):
    s = SEQ
    nt = bias_tabs.shape[1]
    heads = DIL_HEADS
    return pl.pallas_call(
        _attn_kernel,
        grid=(BATCH, heads),
        in_specs=[
            pl.BlockSpec((s, DIL_DH), lambda b, h: (b, h)),
            pl.BlockSpec((s, DIL_DH), lambda b, h: (b, heads + h)),
            pl.BlockSpec((s, DIL_DH), lambda b, h: (b, h)),
            pl.BlockSpec((None, nt, DIL_BLOCK, 2 * DIL_BLOCK), lambda b, h: (h, 0, 0, 0)),
        ],
        out_specs=pl.BlockSpec((s, DIL_DH), lambda b, h: (b, h)),
        out_shape=jax.ShapeDtypeStruct((M_TOK, DIL_W), BF16),
        scratch_shapes=[
            pltpu.VMEM((s, DIL_DH), F32),
            pltpu.VMEM((s, DIL_DH), F32),
            pltpu.VMEM((s, DIL_DH), F32),
            pltpu.VMEM((s, DIL_DH), F32),
            pltpu.VMEM((s, LANE), F32),
            pltpu.VMEM((s, LANE), F32),
        ],
        compiler_params=_cparams(("parallel", "parallel")),
        name="dilated_attention",
    )(qk_p, qk_p, v_p, bias_tabs)


def _to_class_order(a):
    w = a.shape[1]
    return a.reshape(BATCH, CLASS_LEN, N_CLASS, w).transpose(0, 2, 1, 3).reshape(M_TOK, w)


def _to_token_order(a):
    w = a.shape[1]
    return a.reshape(BATCH, N_CLASS, CLASS_LEN, w).transpose(0, 2, 1, 3).reshape(M_TOK, w)


def _outproj_kernel(oa_ref, ob_ref, x_ref, w_ref, g_ref, x1_ref, h2_ref, wb_s):
    @pl.when(pl.program_id(0) == 0)
    def _():
        wb_s[...] = w_ref[...].astype(BF16)

    oa, ob = oa_ref[...], ob_ref[...]
    d = x_ref.shape[1]
    ssq = jnp.zeros((x_ref.shape[0], 1), F32)
    for n in range(d // OUT_CHUNK):
        cs = slice(n * OUT_CHUNK, (n + 1) * OUT_CHUNK)
        acc = jnp.dot(oa, wb_s[0:VA_W, cs], preferred_element_type=F32)
        acc = acc + jnp.dot(ob, wb_s[VA_W:, cs], preferred_element_type=F32)
        x1 = x_ref[:, cs] + acc
        x1_ref[:, cs] = x1
        ssq = ssq + jnp.sum(x1 * x1, axis=-1, keepdims=True)
    scale = lax.rsqrt(ssq * (1.0 / d) + RMS_EPS)
    h2_ref[...] = (x1_ref[...] * scale * g_ref[...]).astype(h2_ref.dtype)


def _outproj(out_a, out_b, x, w_out, layer, norm2_g):
    m, d = x.shape
    tm = TM_OUT
    row = lambda w: pl.BlockSpec((tm, w), lambda i: (i, 0))
    return pl.pallas_call(
        _outproj_kernel,
        grid=(m // tm,),
        in_specs=[row(VA_W), row(DIL_W), row(d),
                  pl.BlockSpec((None, d, d), lambda i: (layer, 0, 0), pipeline_mode=pl.Buffered(1)),
                  pl.BlockSpec((1, d), lambda i: (0, 0))],
        out_specs=[row(d), row(d)],
        out_shape=[jax.ShapeDtypeStruct((m, d), F32), jax.ShapeDtypeStruct((m, d), BF16)],
        scratch_shapes=[pltpu.VMEM((d, d), BF16)],
        compiler_params=_cparams(("arbitrary",)),
        name="outproj_norm2",
    )(out_a, out_b, x, w_out, norm2_g.reshape(1, d))


def _ffn_kernel(h_ref, x_hbm, wg_ref, wu_ref, wd_ref, o_ref, x_sem):
    i, j = pl.program_id(0), pl.program_id(1)
    tm = o_ref.shape[0]

    def hidden():
        h = h_ref[...]
        gate = jnp.dot(h, wg_ref[...].astype(BF16), preferred_element_type=F32)
        up = jnp.dot(h, wu_ref[...].astype(BF16), preferred_element_type=F32)
        return (gate * jax.nn.sigmoid(gate) * up).astype(BF16)

    def down(act):
        return jnp.dot(act, wd_ref[...].astype(BF16), preferred_element_type=F32)

    @pl.when(j == 0)
    def _():
        rows = pl.ds(pl.multiple_of(i * tm, tm), tm)
        residual = pltpu.make_async_copy(x_hbm.at[rows, :], o_ref, x_sem)
        residual.start()
        act = hidden()
        residual.wait()
        o_ref[...] += down(act)

    @pl.when(j > 0)
    def _():
        o_ref[...] += down(hidden())


def _ffn(h2, x1, w_gate, w_up, w_down, layer):
    m, d = x1.shape
    f = w_gate.shape[2]
    tm, tf = TM_FFN, TF_FFN
    return pl.pallas_call(
        _ffn_kernel,
        grid=(m // tm, f // tf),
        in_specs=[pl.BlockSpec((tm, d), lambda i, j: (i, 0)),
                  pl.BlockSpec(memory_space=pl.ANY),
                  pl.BlockSpec((None, d, tf), lambda i, j: (layer, 0, j)),
                  pl.BlockSpec((None, d, tf), lambda i, j: (layer, 0, j)),
                  pl.BlockSpec((None, tf, d), lambda i, j: (layer, j, 0))],
        out_specs=pl.BlockSpec((tm, d), lambda i, j: (i, 0)),
        out_shape=jax.ShapeDtypeStruct((m, d), F32),
        scratch_shapes=[pltpu.SemaphoreType.DMA(())],
        compiler_params=_cparams(("parallel", "arbitrary")),
        name="swiglu_ffn",
    )(h2, x1, w_gate, w_up, w_down)


def kernel(x, norm1_g, w_in, gla_gate_w2, gla_gate_b, gla_onorm_g, q_norm_g, k_norm_g, rel_bias,
           w_out, norm2_g, w_gate, w_up, w_down):
    bsz, s, d = x.shape
    assert (bsz, s, d) == (BATCH, SEQ, D_MODEL)
    bias_tabs = _bias_tables(rel_bias.astype(F32), jnp.asarray(_attention_tables()))
    w_in_t = jnp.swapaxes(w_in, 1, 2)

    xf = x.reshape(M_TOK, D_MODEL)
    for l in range(DEPTH):
        w_gate2 = jnp.pad(gla_gate_w2[l], ((0, GATE_PAD - GLA_GATE_RANK), (0, 0))).astype(BF16)
        qk_gain = jnp.concatenate([jnp.tile(q_norm_g[l], DIL_HEADS) * (DIL_DH ** -0.5 * LOG2E),
                                   jnp.tile(k_norm_g[l], DIL_HEADS)]).reshape(1, 2 * DIL_W).astype(F32)

        h, log_g = _norm_gate(xf, norm1_g[l], w_in_t, l, w_gate2,
                              gla_gate_b[l].reshape(1, QA_W).astype(F32))
        h_p = _to_class_order(h)
        pa = _proj_wt(h, w_in_t, l, 0, PA_W, name="proj_gla")
        h_p, pa = lax.optimization_barrier((h_p, pa))
        qk_p = _proj_wt(h_p, w_in_t, l, PB_OFF, 2 * DIL_W, gain=qk_gain, name="proj_qknorm")
        v_p = _proj_wt(h_p, w_in_t, l, PB_OFF + 2 * DIL_W, DIL_W, name="proj_v")
        out_b_p = _attention(qk_p, v_p, bias_tabs)
        out_b_p, pa = lax.optimization_barrier((out_b_p, pa))
        out_b = _to_token_order(out_b_p)
        out_a = _gla(pa, log_g, gla_onorm_g[l])

        x1, h2 = _outproj(out_a, out_b, xf, w_out, l, norm2_g[l])
        xf = _ffn(h2, x1, w_gate, w_up, w_down, l)
    return xf.reshape(bsz, s, d)
```

```python
import functools
import math

import numpy as np
import jax
import jax.numpy as jnp
from jax import lax
from jax.experimental import pallas as pl
from jax.experimental.pallas import tpu as pltpu

D_MODEL = 2048
BATCH = 4
SEQ = 2048
DEPTH = 2
M_TOK = BATCH * SEQ

GLA_HEADS = 4
GLA_DV = 256
GLA_DK = 128
GLA_GATE_RANK = 16
GLA_GATE_TAU = 16.0
GLA_CHUNK = 64
DIL_HEADS = 8
DIL_DH = 128
DIL_PATTERNS = ((128, 1), (512, 4), (2048, 16))
DIL_BLOCK = 128
REL_BUCKETS = 32
REL_MAX_DIST = 2048
FFN_HIDDEN = 5632
RMS_EPS = 1e-6

QA_W = GLA_HEADS * GLA_DK
VA_W = GLA_HEADS * GLA_DV
DIL_W = DIL_HEADS * DIL_DH
PA_QA, PA_KA, PA_VA, PA_RA = 0, QA_W, 2 * QA_W, 2 * QA_W + VA_W
PA_W = PA_RA + VA_W
GA_OFF = PA_W
PB_OFF = GA_OFF + GLA_GATE_RANK
PB_W = 3 * DIL_W
LANE = 128
MXU_N = 256
F32_ROWS = 8
GATE_PAD = LANE

N_CLASS = 16
CLASS_LEN = SEQ // N_CLASS

BF16 = jnp.bfloat16
F32 = jnp.float32

VMEM_LIMIT = 56 * 1024 * 1024

TM_NORM = 512
TM_PROJ, TN_PROJ = 1024, 1024
TM_OUT, OUT_CHUNK = 512, 512
TM_FFN, TF_FFN = 1024, 512
GLA_HEADS_PER_STEP = 2
GLA_GROUP = 8
ATT_GROUP = 16
LOG2E = math.log2(math.e)


def _cparams(sem):
    return pltpu.CompilerParams(dimension_semantics=sem, vmem_limit_bytes=VMEM_LIMIT)


NT_DIMS = (((1,), (1,)), ((), ()))


def _norm_gate_kernel(x_ref, g_ref, w1_ref, w2_ref, b_ref, h_ref, lg_ref):
    x = x_ref[...]
    ms = jnp.mean(x * x, axis=-1, keepdims=True)
    h = (x * lax.rsqrt(ms + RMS_EPS) * g_ref[...]).astype(BF16)
    h_ref[...] = h
    w1 = w1_ref[...].astype(BF16)
    w1 = jnp.concatenate([w1, jnp.zeros((GATE_PAD - w1.shape[0], w1.shape[1]), BF16)], axis=0)
    ga = lax.dot_general(h, w1, NT_DIMS, preferred_element_type=F32)
    pre = jnp.dot(ga.astype(BF16), w2_ref[...], preferred_element_type=F32) + b_ref[...]
    y = pre * LOG2E
    lg_ref[...] = (jnp.minimum(y, 0.0) - jnp.log2(1.0 + jnp.exp2(-jnp.abs(y)))) * (1.0 / GLA_GATE_TAU)


def _norm_gate(x, g, w_in_t, layer, w2, b):
    m, d = x.shape
    n = w2.shape[1]
    assert GA_OFF % GLA_GATE_RANK == 0
    return pl.pallas_call(
        _norm_gate_kernel,
        grid=(m // TM_NORM,),
        in_specs=[pl.BlockSpec((TM_NORM, d), lambda i: (i, 0)),
                  pl.BlockSpec((1, d), lambda i: (0, 0)),
                  pl.BlockSpec((None, GLA_GATE_RANK, d), lambda i: (layer, GA_OFF // GLA_GATE_RANK, 0)),
                  pl.BlockSpec((GATE_PAD, n), lambda i: (0, 0)),
                  pl.BlockSpec((1, n), lambda i: (0, 0))],
        out_specs=[pl.BlockSpec((TM_NORM, d), lambda i: (i, 0)),
                   pl.BlockSpec((TM_NORM, n), lambda i: (i, 0))],
        out_shape=[jax.ShapeDtypeStruct((m, d), BF16), jax.ShapeDtypeStruct((m, n), F32)],
        compiler_params=_cparams(("parallel",)),
        name="rmsnorm_gate",
    )(x, g.reshape(1, d), w_in_t, w2, b)


def _proj_wt_kernel(h_ref, w_ref, *rest, norm_tiles, n_tiles):
    g_ref = rest[0] if norm_tiles else None
    o_ref, wb_s = rest[-2:]

    @pl.when(pl.program_id(1) == 0)
    def _():
        wb_s[...] = w_ref[0].astype(BF16)

    def tile(qk_norm):
        h = h_ref[...]
        for n in range(o_ref.shape[1] // MXU_N):
            acc = lax.dot_general(h, wb_s[n * MXU_N:(n + 1) * MXU_N, :], NT_DIMS, preferred_element_type=F32)
            if not qk_norm:
                o_ref[:, n * MXU_N:(n + 1) * MXU_N] = acc.astype(o_ref.dtype)
                continue
            for hh in range(MXU_N // DIL_DH):
                cs = slice(n * MXU_N + hh * DIL_DH, n * MXU_N + (hh + 1) * DIL_DH)
                a = acc[:, hh * DIL_DH:(hh + 1) * DIL_DH]
                ms = jnp.mean(a * a, axis=-1, keepdims=True)
                o_ref[:, cs] = (a * lax.rsqrt(ms + RMS_EPS) * g_ref[:, cs]).astype(o_ref.dtype)

    if norm_tiles in (0, n_tiles):
        tile(norm_tiles > 0)
    else:
        pl.when(pl.program_id(0) < norm_tiles)(functools.partial(tile, True))
        pl.when(pl.program_id(0) >= norm_tiles)(functools.partial(tile, False))


def _proj_wt(h, w_in_t, layer, col0, n, gain=None, name="proj"):
    m, k = h.shape
    norm_tiles = 0 if gain is None else gain.shape[1] // TN_PROJ
    in_specs = [pl.BlockSpec((TM_PROJ, k), lambda j, i: (i, 0)),
                pl.BlockSpec((pl.Element(1), pl.Element(TN_PROJ), pl.Element(k)),
                             lambda j, i: (layer, pl.multiple_of(col0 + j * TN_PROJ, F32_ROWS), 0))]
    operands = [h, w_in_t]
    if norm_tiles:
        in_specs.append(pl.BlockSpec((1, TN_PROJ), lambda j, i: (0, jnp.minimum(j, norm_tiles - 1))))
        operands.append(gain)
    return pl.pallas_call(
        functools.partial(_proj_wt_kernel, norm_tiles=norm_tiles, n_tiles=n // TN_PROJ),
        grid=(n // TN_PROJ, m // TM_PROJ),
        in_specs=in_specs,
        out_specs=pl.BlockSpec((TM_PROJ, TN_PROJ), lambda j, i: (i, j)),
        out_shape=jax.ShapeDtypeStruct((m, n), BF16),
        scratch_shapes=[pltpu.VMEM((TN_PROJ, k), BF16)],
        compiler_params=_cparams(("parallel", "arbitrary")),
        name=name,
    )(*operands)


def _gla_kernel(q_ref, k_ref, v_ref, r_ref, lg_ref, gn_ref, o_ref,
                qt_s, kt_s, kd_s, dec_s, st_s, o_s):
    c = GLA_CHUNK
    part_rows = GLA_GROUP * c
    tn_dims = (((0,), (0,)), ((), ()))
    ri = lax.broadcasted_iota(jnp.int32, (c, c), 0)
    ci = lax.broadcasted_iota(jnp.int32, (c, c), 1)
    causal = ri >= ci
    row_in_chunk = lax.broadcasted_iota(jnp.int32, (part_rows, GLA_DK), 0) & (c - 1)

    def chunk_last(a):
        a3 = a.reshape(GLA_GROUP, c, GLA_DK)
        return jnp.broadcast_to(a3[:, c - 1:c, :], a3.shape).reshape(part_rows, GLA_DK)

    def prologue(rows, kc):
        b = lg_ref[rows, kc]
        shift = 1
        while shift < c:
            b = b + jnp.where(row_in_chunk >= shift, pltpu.roll(b, shift, axis=0), 0.0)
            shift *= 2
        eb = jnp.exp2(b)
        q = q_ref[rows, kc].astype(F32) * (GLA_DK ** -0.5)
        k = k_ref[rows, kc].astype(F32)
        qt_s[rows, kc] = (q * eb).astype(BF16)
        kt_s[rows, kc] = (k * jnp.exp2(-b)).astype(BF16)
        kd_s[rows, kc] = (k * jnp.exp2(chunk_last(b) - b)).astype(BF16)
        dec_s[rows, kc] = chunk_last(eb)

    def chunks(first_row, kc, vc):
        starts = [first_row + j * c for j in range(GLA_GROUP)]
        rows = [slice(r0, r0 + c) for r0 in starts]
        qts = [qt_s[r, kc] for r in rows]
        vs = [v_ref[r, vc] for r in rows]
        attn = [lax.dot_general(qt, kt_s[r, kc], NT_DIMS, preferred_element_type=F32)
                for qt, r in zip(qts, rows)]
        cs_t = [lax.dot_general(v, kd_s[r, kc], tn_dims, preferred_element_type=F32)
                for v, r in zip(vs, rows)]
        attn = [jnp.where(causal, a, 0.0).astype(BF16) for a in attn]
        o_intra = [jnp.dot(a, v, preferred_element_type=F32) for a, v in zip(attn, vs)]
        st = st_s[vc, :]
        states = []
        for r0, cs in zip(starts, cs_t):
            states.append(st.astype(BF16))
            st = st * dec_s[r0:r0 + 1, kc] + cs
        st_s[vc, :] = st
        o_inter = [lax.dot_general(qt, sb, NT_DIMS, preferred_element_type=F32)
                   for qt, sb in zip(qts, states)]
        for r, a, b_ in zip(rows, o_intra, o_inter):
            o_s[r, vc] = a + b_

    def epilogue(rows, vc):
        o = o_s[rows, vc]
        ms = jnp.mean(o * o, axis=-1, keepdims=True)
        y = o * lax.rsqrt(ms + RMS_EPS) * gn_ref[...]
        r = r_ref[rows, vc].astype(F32)
        o_ref[rows, vc] = (y * (r * jax.nn.sigmoid(r))).astype(o_ref.dtype)

    st_s[...] = jnp.zeros_like(st_s)
    for hh in range(GLA_HEADS_PER_STEP):
        kc = slice(hh * GLA_DK, (hh + 1) * GLA_DK)
        vc = slice(hh * GLA_DV, (hh + 1) * GLA_DV)
        for first_row in range(0, SEQ, part_rows):
            rows = slice(first_row, first_row + part_rows)
            prologue(rows, kc)
            chunks(first_row, kc, vc)
            epilogue(rows, vc)


def _gla(pa, log_g, onorm_g):
    s = SEQ
    kw, vw = GLA_HEADS_PER_STEP * GLA_DK, GLA_HEADS_PER_STEP * GLA_DV
    return pl.pallas_call(
        _gla_kernel,
        grid=(BATCH, GLA_HEADS // GLA_HEADS_PER_STEP),
        in_specs=[
            pl.BlockSpec((s, kw), lambda b, h: (b, PA_QA // kw + h)),
            pl.BlockSpec((s, kw), lambda b, h: (b, PA_KA // kw + h)),
            pl.BlockSpec((s, vw), lambda b, h: (b, PA_VA // vw + h)),
            pl.BlockSpec((s, vw), lambda b, h: (b, PA_RA // vw + h)),
            pl.BlockSpec((s, kw), lambda b, h: (b, h)),
            pl.BlockSpec((1, GLA_DV), lambda b, h: (0, 0)),
        ],
        out_specs=pl.BlockSpec((s, vw), lambda b, h: (b, h)),
        out_shape=jax.ShapeDtypeStruct((M_TOK, VA_W), BF16),
        scratch_shapes=[
            pltpu.VMEM((s, kw), BF16),
            pltpu.VMEM((s, kw), BF16),
            pltpu.VMEM((s, kw), BF16),
            pltpu.VMEM((s, kw), F32),
            pltpu.VMEM((vw, GLA_DK), F32),
            pltpu.VMEM((s, vw), F32),
        ],
        compiler_params=_cparams(("parallel", "parallel")),
        name="gla_mixer",
    )(pa, pa, pa, pa, log_g, onorm_g.reshape(1, GLA_DV))


def _t5_bucket(dist):
    max_exact = REL_BUCKETS // 2
    safe = np.maximum(dist, 1)
    large = max_exact + (np.log(safe / max_exact) / np.log(REL_MAX_DIST / max_exact)
                         * (REL_BUCKETS - max_exact)).astype(np.int64)
    large = np.minimum(large, REL_BUCKETS - 1)
    return np.where(dist < max_exact, dist, large).astype(np.int32)


ATT_B3, ATT_B2_FIRST, ATT_B2_REST, ATT_B1_FIRST, ATT_B1_REST = range(5)
B2_SLABS, B2_Q_ROWS = 4, 32
B1_SLABS, B1_Q_ROWS = 16, 8
MASKED_BUCKET = -1


def _attention_tables():
    far = 10 ** 6

    def slab_tokens(n_slabs, rows, first_pos, class_step):
        pos = first_pos + np.arange(rows)
        return (N_CLASS * pos[None, :] + class_step * np.arange(n_slabs)[:, None]).reshape(-1)

    specs = []
    a = np.arange(CLASS_LEN)
    specs.append((N_CLASS * a, np.concatenate([N_CLASS * a, np.full(CLASS_LEN, far)]), DIL_PATTERNS[2][0]))
    for first in (True, False):
        tq = slab_tokens(B2_SLABS, B2_Q_ROWS, 0, 4)
        tk = slab_tokens(B2_SLABS, 2 * B2_Q_ROWS, 0 if first else -B2_Q_ROWS, 4)
        specs.append((tq, tk, DIL_PATTERNS[1][0]))
    for first in (True, False):
        tq = slab_tokens(B1_SLABS, B1_Q_ROWS, 0, 1)
        tk = slab_tokens(B1_SLABS, 2 * B1_Q_ROWS, 0 if first else -B1_Q_ROWS, 1)
        specs.append((tq, tk, DIL_PATTERNS[0][0]))
    buckets = []
    for tq, tk, window in specs:
        dist = tq[:, None] - tk[None, :]
        in_band = (dist >= 0) & (dist <= window)
        buckets.append(np.where(in_band, _t5_bucket(np.clip(dist, 0, None)), MASKED_BUCKET))
    return np.stack(buckets).astype(np.int32)


def _bias_kernel(rb_ref, bk_ref, o_ref):
    bk = bk_ref[...]
    for h in range(DIL_HEADS):
        acc = jnp.full(bk.shape, -jnp.inf, F32)
        for bucket in range(REL_BUCKETS):
            acc = jnp.where(bk == bucket, rb_ref[bucket, h] * LOG2E, acc)
        o_ref[h] = acc


def _bias_tables(rel_bias, buckets):
    nt, q, q2 = buckets.shape
    return pl.pallas_call(
        _bias_kernel,
        grid=(nt,),
        in_specs=[pl.BlockSpec(memory_space=pltpu.SMEM),
                  pl.BlockSpec((None, q, q2), lambda i: (i, 0, 0))],
        out_specs=pl.BlockSpec((DIL_HEADS, None, q, q2), lambda i: (0, i, 0, 0)),
        out_shape=jax.ShapeDtypeStruct((DIL_HEADS, nt, q, q2), F32),
        compiler_params=_cparams(("parallel",)),
        name="t5_bias_tables",
    )(rel_bias, buckets)


def _gather(ref, slabs):
    return jnp.concatenate([ref[pl.ds(s, n), :] for s, n in slabs], axis=0)


def _scatter(ref, slabs, val):
    off = 0
    for s, n in slabs:
        ref[pl.ds(s, n), :] = val[off:off + n]
        off += n


def _attn_kernel(q_ref, k_ref, v_ref, bias_ref, o_ref,
                 q32_s, k32_s, v32_s, acc_s, m_s, l_s):
    qb = DIL_BLOCK
    q32_s[...] = q_ref[...].astype(F32)
    k32_s[...] = k_ref[...].astype(F32)
    v32_s[...] = v_ref[...].astype(F32)

    def b3_block(u):
        slab = [(u * CLASS_LEN, CLASS_LEN)]
        return (q_ref, k_ref, v_ref, ATT_B3, CLASS_LEN, slab, slab)

    def b2_block(e, kk):
        q0 = kk * B2_Q_ROWS
        k0 = max(q0 - B2_Q_ROWS, 0)
        bases = [(DIL_PATTERNS[1][1] * c + e) * CLASS_LEN for c in range(B2_SLABS)]
        return (q_ref, k_ref, v_ref, ATT_B2_FIRST if kk == 0 else ATT_B2_REST, 2 * qb,
                [(base + q0, B2_Q_ROWS) for base in bases], [(base + k0, 2 * B2_Q_ROWS) for base in bases])

    def b1_block(kk):
        q0 = kk * B1_Q_ROWS
        k0 = max(q0 - B1_Q_ROWS, 0)
        return (q32_s, k32_s, v32_s, ATT_B1_FIRST if kk == 0 else ATT_B1_REST, 2 * qb,
                [(u * CLASS_LEN + q0, B1_Q_ROWS) for u in range(B1_SLABS)],
                [(u * CLASS_LEN + k0, 2 * B1_Q_ROWS) for u in range(B1_SLABS)])

    def logits_of(group):
        return [lax.dot_general(_gather(qs, q_sl).astype(BF16), _gather(ks, k_sl).astype(BF16),
                                NT_DIMS, preferred_element_type=F32)
                for qs, ks, _, _, _, q_sl, k_sl in group]

    def update(group, logits, init):
        olds = [None if init else (_gather(m_s, q_sl), _gather(l_s, q_sl), _gather(acc_s, q_sl))
                for *_, q_sl, _ in group]
        probs, stats = [], []
        for s, (_, _, _, tab, nk, _, _), old in zip(logits, group, olds):
            s = s + bias_ref[tab, :, 0:nk]
            m_blk = jnp.broadcast_to(jnp.max(s, axis=-1, keepdims=True), (qb, LANE))
            if init:
                m_new, alpha = m_blk, None
            else:
                m_new = jnp.maximum(old[0], m_blk)
                alpha = jnp.exp2(old[0] - m_new)
            p = jnp.exp2(s - jnp.concatenate([m_new] * (nk // LANE), axis=1))
            probs.append(p.astype(BF16))
            stats.append((m_new, alpha))
        pvs = [jnp.dot(p, jnp.concatenate([_gather(vs, k_sl).astype(BF16), jnp.ones((nk, LANE), BF16)], axis=1),
                       preferred_element_type=F32)
               for p, (_, _, vs, _, nk, _, k_sl) in zip(probs, group)]
        for pvl, (m_new, alpha), old, (*_, q_sl, _) in zip(pvs, stats, olds, group):
            pv, l_blk = pvl[:, :DIL_DH], pvl[:, DIL_DH:]
            _scatter(m_s, q_sl, m_new)
            _scatter(l_s, q_sl, l_blk if init else alpha * old[1] + l_blk)
            _scatter(acc_s, q_sl, pv if init else alpha * old[2] + pv)

    n_res = DIL_PATTERNS[1][1]
    blocks = ([(b3_block(u), True) for u in range(N_CLASS)]
              + [(b2_block(e, kk), False) for kk in range(CLASS_LEN // B2_Q_ROWS) for e in range(n_res)]
              + [(b1_block(kk), False) for kk in range(CLASS_LEN // B1_Q_ROWS)])
    assert N_CLASS % ATT_GROUP == 0 and len(blocks) % ATT_GROUP == 0
    groups = [blocks[g:g + ATT_GROUP] for g in range(0, len(blocks), ATT_GROUP)]
    for grp in groups:
        blks, init = [b for b, _ in grp], grp[0][1]
        update(blks, logits_of(blks), init)

    o_ref[...] = (acc_s[...] / l_s[...]).astype(o_ref.dtype)


def _attention(qkv_p, bias_tabs):
    s = SEQ
    nt = bias_tabs.shape[1]
    heads = DIL_HEADS
    return pl.pallas_call(
        _attn_kernel,
        grid=(BATCH, heads),
        in_specs=[
            pl.BlockSpec((s, DIL_DH), lambda b, h: (b, h)),
            pl.BlockSpec((s, DIL_DH), lambda b, h: (b, heads + h)),
            pl.BlockSpec((s, DIL_DH), lambda b, h: (b, 2 * heads + h)),
            pl.BlockSpec((None, nt, DIL_BLOCK, 2 * DIL_BLOCK), lambda b, h: (h, 0, 0, 0)),
        ],
        out_specs=pl.BlockSpec((s, DIL_DH), lambda b, h: (b, h)),
        out_shape=jax.ShapeDtypeStruct((M_TOK, DIL_W), BF16),
        scratch_shapes=[
            pltpu.VMEM((s, DIL_DH), F32),
            pltpu.VMEM((s, DIL_DH), F32),
            pltpu.VMEM((s, DIL_DH), F32),
            pltpu.VMEM((s, DIL_DH), F32),
            pltpu.VMEM((s, LANE), F32),
            pltpu.VMEM((s, LANE), F32),
        ],
        compiler_params=_cparams(("parallel", "parallel")),
        name="dilated_attention",
    )(qkv_p, qkv_p, qkv_p, bias_tabs)


def _to_class_order(a):
    w = a.shape[1]
    return a.reshape(BATCH, CLASS_LEN, N_CLASS, w).transpose(0, 2, 1, 3).reshape(M_TOK, w)


def _to_token_order(a):
    w = a.shape[1]
    return a.reshape(BATCH, N_CLASS, CLASS_LEN, w).transpose(0, 2, 1, 3).reshape(M_TOK, w)


def _outproj_kernel(oa_ref, ob_ref, x_ref, w_ref, g_ref, x1_ref, h2_ref, wb_s):
    @pl.when(pl.program_id(0) == 0)
    def _():
        wb_s[...] = w_ref[...].astype(BF16)

    oa, ob = oa_ref[...], ob_ref[...]
    d = x_ref.shape[1]
    ssq = jnp.zeros((x_ref.shape[0], 1), F32)
    for n in range(d // OUT_CHUNK):
        cs = slice(n * OUT_CHUNK, (n + 1) * OUT_CHUNK)
        acc = jnp.dot(oa, wb_s[0:VA_W, cs], preferred_element_type=F32)
        acc = acc + jnp.dot(ob, wb_s[VA_W:, cs], preferred_element_type=F32)
        x1 = x_ref[:, cs] + acc
        x1_ref[:, cs] = x1
        ssq = ssq + jnp.sum(x1 * x1, axis=-1, keepdims=True)
    scale = lax.rsqrt(ssq * (1.0 / d) + RMS_EPS)
    h2_ref[...] = (x1_ref[...] * scale * g_ref[...]).astype(h2_ref.dtype)


def _outproj(out_a, out_b, x, w_out, layer, norm2_g):
    m, d = x.shape
    tm = TM_OUT
    row = lambda w: pl.BlockSpec((tm, w), lambda i: (i, 0))
    return pl.pallas_call(
        _outproj_kernel,
        grid=(m // tm,),
        in_specs=[row(VA_W), row(DIL_W), row(d),
                  pl.BlockSpec((None, d, d), lambda i: (layer, 0, 0), pipeline_mode=pl.Buffered(1)),
                  pl.BlockSpec((1, d), lambda i: (0, 0))],
        out_specs=[row(d), row(d)],
        out_shape=[jax.ShapeDtypeStruct((m, d), F32), jax.ShapeDtypeStruct((m, d), BF16)],
        scratch_shapes=[pltpu.VMEM((d, d), BF16)],
        compiler_params=_cparams(("arbitrary",)),
        name="outproj_norm2",
    )(out_a, out_b, x, w_out, norm2_g.reshape(1, d))


def _ffn_kernel(h_ref, x_hbm, wg_ref, wu_ref, wd_ref, o_ref, x_sem):
    i, j = pl.program_id(0), pl.program_id(1)
    tm = o_ref.shape[0]

    def hidden():
        h = h_ref[...]
        gate = jnp.dot(h, wg_ref[...].astype(BF16), preferred_element_type=F32)
        up = jnp.dot(h, wu_ref[...].astype(BF16), preferred_element_type=F32)
        return (gate * jax.nn.sigmoid(gate) * up).astype(BF16)

    def down(act):
        return jnp.dot(act, wd_ref[...].astype(BF16), preferred_element_type=F32)

    @pl.when(j == 0)
    def _():
        rows = pl.ds(pl.multiple_of(i * tm, tm), tm)
        residual = pltpu.make_async_copy(x_hbm.at[rows, :], o_ref, x_sem)
        residual.start()
        act = hidden()
        residual.wait()
        o_ref[...] += down(act)

    @pl.when(j > 0)
    def _():
        o_ref[...] += down(hidden())


def _ffn(h2, x1, w_gate, w_up, w_down, layer):
    m, d = x1.shape
    f = w_gate.shape[2]
    tm, tf = TM_FFN, TF_FFN
    return pl.pallas_call(
        _ffn_kernel,
        grid=(m // tm, f // tf),
        in_specs=[pl.BlockSpec((tm, d), lambda i, j: (i, 0)),
                  pl.BlockSpec(memory_space=pl.ANY),
                  pl.BlockSpec((None, d, tf), lambda i, j: (layer, 0, j)),
                  pl.BlockSpec((None, d, tf), lambda i, j: (layer, 0, j)),
                  pl.BlockSpec((None, tf, d), lambda i, j: (layer, j, 0))],
        out_specs=pl.BlockSpec((tm, d), lambda i, j: (i, 0)),
        out_shape=jax.ShapeDtypeStruct((m, d), F32),
        scratch_shapes=[pltpu.SemaphoreType.DMA(())],
        compiler_params=_cparams(("parallel", "arbitrary")),
        name="swiglu_ffn",
    )(h2, x1, w_gate, w_up, w_down)


def kernel(x, norm1_g, w_in, gla_gate_w2, gla_gate_b, gla_onorm_g, q_norm_g, k_norm_g, rel_bias,
           w_out, norm2_g, w_gate, w_up, w_down):
    bsz, s, d = x.shape
    assert (bsz, s, d) == (BATCH, SEQ, D_MODEL)
    bias_tabs = _bias_tables(rel_bias.astype(F32), jnp.asarray(_attention_tables()))
    w_in_t = jnp.swapaxes(w_in, 1, 2)

    xf = x.reshape(M_TOK, D_MODEL)
    for l in range(DEPTH):
        w_gate2 = jnp.pad(gla_gate_w2[l], ((0, GATE_PAD - GLA_GATE_RANK), (0, 0))).astype(BF16)
        qk_gain = jnp.concatenate([jnp.tile(q_norm_g[l], DIL_HEADS) * (DIL_DH ** -0.5 * LOG2E),
                                   jnp.tile(k_norm_g[l], DIL_HEADS)]).reshape(1, 2 * DIL_W).astype(F32)

        h, log_g = _norm_gate(xf, norm1_g[l], w_in_t, l, w_gate2,
                              gla_gate_b[l].reshape(1, QA_W).astype(F32))
        h_p = _to_class_order(h)
        pa = _proj_wt(h, w_in_t, l, 0, PA_W, name="proj_gla")
        h_p, pa = lax.optimization_barrier((h_p, pa))
        qkv_p = _proj_wt(h_p, w_in_t, l, PB_OFF, PB_W, gain=qk_gain, name="proj_qkv")
        out_b_p = _attention(qkv_p, bias_tabs)
        out_b_p, pa = lax.optimization_barrier((out_b_p, pa))
        out_b = _to_token_order(out_b_p)
        out_a = _gla(pa, log_g, gla_onorm_g[l])

        x1, h2 = _outproj(out_a, out_b, xf, w_out, l, norm2_g[l])
        xf = _ffn(h2, x1, w_gate, w_up, w_down, l)
    return xf.reshape(bsz, s, d)
```

```python
import functools
import math

import numpy as np
import jax
import jax.numpy as jnp
from jax import lax
from jax.experimental import pallas as pl
from jax.experimental.pallas import tpu as pltpu

D_MODEL = 2048
BATCH = 4
SEQ = 2048
DEPTH = 2
M_TOK = BATCH * SEQ

GLA_HEADS = 4
GLA_DV = 256
GLA_DK = 128
GLA_GATE_RANK = 16
GLA_GATE_TAU = 16.0
GLA_CHUNK = 64
DIL_HEADS = 8
DIL_DH = 128
DIL_PATTERNS = ((128, 1), (512, 4), (2048, 16))
DIL_BLOCK = 128
REL_BUCKETS = 32
REL_MAX_DIST = 2048
FFN_HIDDEN = 5632
RMS_EPS = 1e-6

QA_W = GLA_HEADS * GLA_DK
VA_W = GLA_HEADS * GLA_DV
DIL_W = DIL_HEADS * DIL_DH
PA_QA, PA_KA, PA_VA, PA_RA = 0, QA_W, 2 * QA_W, 2 * QA_W + VA_W
PA_W = PA_RA + VA_W
GA_OFF = PA_W
PB_OFF = GA_OFF + GLA_GATE_RANK
PB_W = 3 * DIL_W
LANE = 128
MXU_N = 256
F32_ROWS = 8
GATE_PAD = LANE

N_CLASS = 16
CLASS_LEN = SEQ // N_CLASS

BF16 = jnp.bfloat16
F32 = jnp.float32

VMEM_LIMIT = 56 * 1024 * 1024

TM_NORM = 1024
TM_PROJ, TN_PROJ = 1024, 1024
TM_OUT, OUT_CHUNK = 512, 512
TM_FFN, TF_FFN = 1024, 512
GLA_HEADS_PER_STEP = 1
GLA_GROUP = 8
ATT_GROUP = 16
LOG2E = math.log2(math.e)


def _cparams(sem):
    return pltpu.CompilerParams(dimension_semantics=sem, vmem_limit_bytes=VMEM_LIMIT)


NT_DIMS = (((1,), (1,)), ((), ()))


def _norm_gate_kernel(x_ref, g_ref, w1_ref, w2_ref, b_ref, h_ref, lg_ref):
    x = x_ref[...]
    ms = jnp.mean(x * x, axis=-1, keepdims=True)
    h = (x * lax.rsqrt(ms + RMS_EPS) * g_ref[...]).astype(BF16)
    h_ref[...] = h
    w1 = w1_ref[...].astype(BF16)
    w1 = jnp.concatenate([w1, jnp.zeros((GATE_PAD - w1.shape[0], w1.shape[1]), BF16)], axis=0)
    ga = lax.dot_general(h, w1, NT_DIMS, preferred_element_type=F32)
    pre = jnp.dot(ga.astype(BF16), w2_ref[...], preferred_element_type=F32) + b_ref[...]
    y = pre * LOG2E
    lg_ref[...] = (jnp.minimum(y, 0.0) - jnp.log2(1.0 + jnp.exp2(-jnp.abs(y)))) * (1.0 / GLA_GATE_TAU)


def _norm_gate(x, g, w_in_t, layer, w2, b):
    m, d = x.shape
    n = w2.shape[1]
    assert GA_OFF % GLA_GATE_RANK == 0
    return pl.pallas_call(
        _norm_gate_kernel,
        grid=(m // TM_NORM,),
        in_specs=[pl.BlockSpec((TM_NORM, d), lambda i: (i, 0)),
                  pl.BlockSpec((1, d), lambda i: (0, 0)),
                  pl.BlockSpec((None, GLA_GATE_RANK, d), lambda i: (layer, GA_OFF // GLA_GATE_RANK, 0)),
                  pl.BlockSpec((GATE_PAD, n), lambda i: (0, 0)),
                  pl.BlockSpec((1, n), lambda i: (0, 0))],
        out_specs=[pl.BlockSpec((TM_NORM, d), lambda i: (i, 0)),
                   pl.BlockSpec((TM_NORM, n), lambda i: (i, 0))],
        out_shape=[jax.ShapeDtypeStruct((m, d), BF16), jax.ShapeDtypeStruct((m, n), F32)],
        compiler_params=_cparams(("parallel",)),
        name="rmsnorm_gate",
    )(x, g.reshape(1, d), w_in_t, w2, b)


def _proj_wt_kernel(h_ref, w_ref, *rest, norm_tiles, n_tiles):
    g_ref = rest[0] if norm_tiles else None
    o_ref, wb_s = rest[-2:]

    @pl.when(pl.program_id(1) == 0)
    def _():
        wb_s[...] = w_ref[0].astype(BF16)

    def tile(qk_norm):
        h = h_ref[...]
        for n in range(o_ref.shape[1] // MXU_N):
            acc = lax.dot_general(h, wb_s[n * MXU_N:(n + 1) * MXU_N, :], NT_DIMS, preferred_element_type=F32)
            if not qk_norm:
                o_ref[:, n * MXU_N:(n + 1) * MXU_N] = acc.astype(o_ref.dtype)
                continue
            for hh in range(MXU_N // DIL_DH):
                cs = slice(n * MXU_N + hh * DIL_DH, n * MXU_N + (hh + 1) * DIL_DH)
                a = acc[:, hh * DIL_DH:(hh + 1) * DIL_DH]
                ms = jnp.mean(a * a, axis=-1, keepdims=True)
                o_ref[:, cs] = (a * lax.rsqrt(ms + RMS_EPS) * g_ref[:, cs]).astype(o_ref.dtype)

    if norm_tiles in (0, n_tiles):
        tile(norm_tiles > 0)
    else:
        pl.when(pl.program_id(0) < norm_tiles)(functools.partial(tile, True))
        pl.when(pl.program_id(0) >= norm_tiles)(functools.partial(tile, False))


def _proj_wt(h, w_in_t, layer, col0, n, gain=None, name="proj"):
    m, k = h.shape
    norm_tiles = 0 if gain is None else gain.shape[1] // TN_PROJ
    in_specs = [pl.BlockSpec((TM_PROJ, k), lambda j, i: (i, 0)),
                pl.BlockSpec((pl.Element(1), pl.Element(TN_PROJ), pl.Element(k)),
                             lambda j, i: (layer, pl.multiple_of(col0 + j * TN_PROJ, F32_ROWS), 0))]
    operands = [h, w_in_t]
    if norm_tiles:
        in_specs.append(pl.BlockSpec((1, TN_PROJ), lambda j, i: (0, jnp.minimum(j, norm_tiles - 1))))
        operands.append(gain)
    return pl.pallas_call(
        functools.partial(_proj_wt_kernel, norm_tiles=norm_tiles, n_tiles=n // TN_PROJ),
        grid=(n // TN_PROJ, m // TM_PROJ),
        in_specs=in_specs,
        out_specs=pl.BlockSpec((TM_PROJ, TN_PROJ), lambda j, i: (i, j)),
        out_shape=jax.ShapeDtypeStruct((m, n), BF16),
        scratch_shapes=[pltpu.VMEM((TN_PROJ, k), BF16)],
        compiler_params=_cparams(("parallel", "arbitrary")),
        name=name,
    )(*operands)


def _gla_kernel(q_ref, k_ref, v_ref, r_ref, lg_ref, gn_ref, o_ref,
                qt_s, kt_s, kd_s, dec_s, st_s, o_s):
    c = GLA_CHUNK
    part_rows = GLA_GROUP * c
    tn_dims = (((0,), (0,)), ((), ()))
    ri = lax.broadcasted_iota(jnp.int32, (c, c), 0)
    ci = lax.broadcasted_iota(jnp.int32, (c, c), 1)
    causal = ri >= ci
    row_in_chunk = lax.broadcasted_iota(jnp.int32, (part_rows, GLA_DK), 0) & (c - 1)

    def chunk_last(a):
        a3 = a.reshape(GLA_GROUP, c, GLA_DK)
        return jnp.broadcast_to(a3[:, c - 1:c, :], a3.shape).reshape(part_rows, GLA_DK)

    def prologue(rows, kc):
        b = lg_ref[rows, kc]
        shift = 1
        while shift < c:
            b = b + jnp.where(row_in_chunk >= shift, pltpu.roll(b, shift, axis=0), 0.0)
            shift *= 2
        eb = jnp.exp2(b)
        q = q_ref[rows, kc].astype(F32) * (GLA_DK ** -0.5)
        k = k_ref[rows, kc].astype(F32)
        qt_s[rows, kc] = (q * eb).astype(BF16)
        kt_s[rows, kc] = (k * jnp.exp2(-b)).astype(BF16)
        kd_s[rows, kc] = (k * jnp.exp2(chunk_last(b) - b)).astype(BF16)
        dec_s[rows, kc] = chunk_last(eb)

    def chunks(first_row, kc, vc):
        starts = [first_row + j * c for j in range(GLA_GROUP)]
        rows = [slice(r0, r0 + c) for r0 in starts]
        qts = [qt_s[r, kc] for r in rows]
        vs = [v_ref[r, vc] for r in rows]
        attn = [lax.dot_general(qt, kt_s[r, kc], NT_DIMS, preferred_element_type=F32)
                for qt, r in zip(qts, rows)]
        cs_t = [lax.dot_general(v, kd_s[r, kc], tn_dims, preferred_element_type=F32)
                for v, r in zip(vs, rows)]
        attn = [jnp.where(causal, a, 0.0).astype(BF16) for a in attn]
        o_intra = [jnp.dot(a, v, preferred_element_type=F32) for a, v in zip(attn, vs)]
        st = st_s[vc, :]
        states = []
        for r0, cs in zip(starts, cs_t):
            states.append(st.astype(BF16))
            st = st * dec_s[r0:r0 + 1, kc] + cs
        st_s[vc, :] = st
        o_inter = [lax.dot_general(qt, sb, NT_DIMS, preferred_element_type=F32)
                   for qt, sb in zip(qts, states)]
        for r, a, b_ in zip(rows, o_intra, o_inter):
            o_s[r, vc] = a + b_

    def epilogue(rows, vc):
        o = o_s[rows, vc]
        ms = jnp.mean(o * o, axis=-1, keepdims=True)
        y = o * lax.rsqrt(ms + RMS_EPS) * gn_ref[...]
        r = r_ref[rows, vc].astype(F32)
        o_ref[rows, vc] = (y * (r * jax.nn.sigmoid(r))).astype(o_ref.dtype)

    st_s[...] = jnp.zeros_like(st_s)
    for hh in range(GLA_HEADS_PER_STEP):
        kc = slice(hh * GLA_DK, (hh + 1) * GLA_DK)
        vc = slice(hh * GLA_DV, (hh + 1) * GLA_DV)
        for first_row in range(0, SEQ, part_rows):
            rows = slice(first_row, first_row + part_rows)
            prologue(rows, kc)
            chunks(first_row, kc, vc)
            epilogue(rows, vc)


def _gla(pa, log_g, onorm_g):
    s = SEQ
    kw, vw = GLA_HEADS_PER_STEP * GLA_DK, GLA_HEADS_PER_STEP * GLA_DV
    return pl.pallas_call(
        _gla_kernel,
        grid=(BATCH, GLA_HEADS // GLA_HEADS_PER_STEP),
        in_specs=[
            pl.BlockSpec((s, kw), lambda b, h: (b, PA_QA // kw + h)),
            pl.BlockSpec((s, kw), lambda b, h: (b, PA_KA // kw + h)),
            pl.BlockSpec((s, vw), lambda b, h: (b, PA_VA // vw + h)),
            pl.BlockSpec((s, vw), lambda b, h: (b, PA_RA // vw + h)),
            pl.BlockSpec((s, kw), lambda b, h: (b, h)),
            pl.BlockSpec((1, GLA_DV), lambda b, h: (0, 0)),
        ],
        out_specs=pl.BlockSpec((s, vw), lambda b, h: (b, h)),
        out_shape=jax.ShapeDtypeStruct((M_TOK, VA_W), BF16),
        scratch_shapes=[
            pltpu.VMEM((s, kw), BF16),
            pltpu.VMEM((s, kw), BF16),
            pltpu.VMEM((s, kw), BF16),
            pltpu.VMEM((s, kw), F32),
            pltpu.VMEM((vw, GLA_DK), F32),
            pltpu.VMEM((s, vw), F32),
        ],
        compiler_params=_cparams(("parallel", "parallel")),
        name="gla_mixer",
    )(pa, pa, pa, pa, log_g, onorm_g.reshape(1, GLA_DV))


def _t5_bucket(dist):
    max_exact = REL_BUCKETS // 2
    safe = np.maximum(dist, 1)
    large = max_exact + (np.log(safe / max_exact) / np.log(REL_MAX_DIST / max_exact)
                         * (REL_BUCKETS - max_exact)).astype(np.int64)
    large = np.minimum(large, REL_BUCKETS - 1)
    return np.where(dist < max_exact, dist, large).astype(np.int32)


ATT_B3, ATT_B2_FIRST, ATT_B2_REST, ATT_B1_FIRST, ATT_B1_REST = range(5)
B2_SLABS, B2_Q_ROWS = 4, 32
B1_SLABS, B1_Q_ROWS = 16, 8
MASKED_BUCKET = -1


def _attention_tables():
    far = 10 ** 6

    def slab_tokens(n_slabs, rows, first_pos, class_step):
        pos = first_pos + np.arange(rows)
        return (N_CLASS * pos[None, :] + class_step * np.arange(n_slabs)[:, None]).reshape(-1)

    specs = []
    a = np.arange(CLASS_LEN)
    specs.append((N_CLASS * a, np.concatenate([N_CLASS * a, np.full(CLASS_LEN, far)]), DIL_PATTERNS[2][0]))
    for first in (True, False):
        tq = slab_tokens(B2_SLABS, B2_Q_ROWS, 0, 4)
        tk = slab_tokens(B2_SLABS, 2 * B2_Q_ROWS, 0 if first else -B2_Q_ROWS, 4)
        specs.append((tq, tk, DIL_PATTERNS[1][0]))
    for first in (True, False):
        tq = slab_tokens(B1_SLABS, B1_Q_ROWS, 0, 1)
        tk = slab_tokens(B1_SLABS, 2 * B1_Q_ROWS, 0 if first else -B1_Q_ROWS, 1)
        specs.append((tq, tk, DIL_PATTERNS[0][0]))
    buckets = []
    for tq, tk, window in specs:
        dist = tq[:, None] - tk[None, :]
        in_band = (dist >= 0) & (dist <= window)
        buckets.append(np.where(in_band, _t5_bucket(np.clip(dist, 0, None)), MASKED_BUCKET))
    return np.stack(buckets).astype(np.int32)


def _bias_kernel(rb_ref, bk_ref, o_ref):
    bk = bk_ref[...]
    for h in range(DIL_HEADS):
        acc = jnp.full(bk.shape, -jnp.inf, F32)
        for bucket in range(REL_BUCKETS):
            acc = jnp.where(bk == bucket, rb_ref[bucket, h] * LOG2E, acc)
        o_ref[h] = acc


def _bias_tables(rel_bias, buckets):
    nt, q, q2 = buckets.shape
    return pl.pallas_call(
        _bias_kernel,
        grid=(nt,),
        in_specs=[pl.BlockSpec(memory_space=pltpu.SMEM),
                  pl.BlockSpec((None, q, q2), lambda i: (i, 0, 0))],
        out_specs=pl.BlockSpec((DIL_HEADS, None, q, q2), lambda i: (0, i, 0, 0)),
        out_shape=jax.ShapeDtypeStruct((DIL_HEADS, nt, q, q2), F32),
        compiler_params=_cparams(("parallel",)),
        name="t5_bias_tables",
    )(rel_bias, buckets)


def _gather(ref, slabs):
    return jnp.concatenate([ref[pl.ds(s, n), :] for s, n in slabs], axis=0)


def _scatter(ref, slabs, val):
    off = 0
    for s, n in slabs:
        ref[pl.ds(s, n), :] = val[off:off + n]
        off += n


def _attn_kernel(q_ref, k_ref, v_ref, bias_ref, o_ref,
                 q32_s, k32_s, v32_s, acc_s, m_s, l_s):
    qb = DIL_BLOCK
    q32_s[...] = q_ref[...].astype(F32)
    k32_s[...] = k_ref[...].astype(F32)
    v32_s[...] = v_ref[...].astype(F32)

    def b3_block(u):
        slab = [(u * CLASS_LEN, CLASS_LEN)]
        return (q_ref, k_ref, v_ref, ATT_B3, CLASS_LEN, slab, slab)

    def b2_block(e, kk):
        q0 = kk * B2_Q_ROWS
        k0 = max(q0 - B2_Q_ROWS, 0)
        bases = [(DIL_PATTERNS[1][1] * c + e) * CLASS_LEN for c in range(B2_SLABS)]
        return (q_ref, k_ref, v_ref, ATT_B2_FIRST if kk == 0 else ATT_B2_REST, 2 * qb,
                [(base + q0, B2_Q_ROWS) for base in bases], [(base + k0, 2 * B2_Q_ROWS) for base in bases])

    def b1_block(kk):
        q0 = kk * B1_Q_ROWS
        k0 = max(q0 - B1_Q_ROWS, 0)
        return (q32_s, k32_s, v32_s, ATT_B1_FIRST if kk == 0 else ATT_B1_REST, 2 * qb,
                [(u * CLASS_LEN + q0, B1_Q_ROWS) for u in range(B1_SLABS)],
                [(u * CLASS_LEN + k0, 2 * B1_Q_ROWS) for u in range(B1_SLABS)])

    def logits_of(group):
        return [lax.dot_general(_gather(qs, q_sl).astype(BF16), _gather(ks, k_sl).astype(BF16),
                                NT_DIMS, preferred_element_type=F32)
                for qs, ks, _, _, _, q_sl, k_sl in group]

    def update(group, logits, init):
        olds = [None if init else (_gather(m_s, q_sl), _gather(l_s, q_sl), _gather(acc_s, q_sl))
                for *_, q_sl, _ in group]
        probs, stats = [], []
        for s, (_, _, _, tab, nk, _, _), old in zip(logits, group, olds):
            s = s + bias_ref[tab, :, 0:nk]
            m_blk = jnp.broadcast_to(jnp.max(s, axis=-1, keepdims=True), (qb, LANE))
            if init:
                m_new, alpha = m_blk, None
            else:
                m_new = jnp.maximum(old[0], m_blk)
                alpha = jnp.exp2(old[0] - m_new)
            p = jnp.exp2(s - jnp.concatenate([m_new] * (nk // LANE), axis=1))
            probs.append(p.astype(BF16))
            stats.append((m_new, alpha))
        pvs = [jnp.dot(p, jnp.concatenate([_gather(vs, k_sl).astype(BF16), jnp.ones((nk, LANE), BF16)], axis=1),
                       preferred_element_type=F32)
               for p, (_, _, vs, _, nk, _, k_sl) in zip(probs, group)]
        for pvl, (m_new, alpha), old, (*_, q_sl, _) in zip(pvs, stats, olds, group):
            pv, l_blk = pvl[:, :DIL_DH], pvl[:, DIL_DH:]
            _scatter(m_s, q_sl, m_new)
            _scatter(l_s, q_sl, l_blk if init else alpha * old[1] + l_blk)
            _scatter(acc_s, q_sl, pv if init else alpha * old[2] + pv)

    n_res = DIL_PATTERNS[1][1]
    blocks = ([(b3_block(u), True) for u in range(N_CLASS)]
              + [(b2_block(e, kk), False) for kk in range(CLASS_LEN // B2_Q_ROWS) for e in range(n_res)]
              + [(b1_block(kk), False) for kk in range(CLASS_LEN // B1_Q_ROWS)])
    assert N_CLASS % ATT_GROUP == 0 and len(blocks) % ATT_GROUP == 0
    groups = [blocks[g:g + ATT_GROUP] for g in range(0, len(blocks), ATT_GROUP)]
    for grp in groups:
        blks, init = [b for b, _ in grp], grp[0][1]
        update(blks, logits_of(blks), init)

    o_ref[...] = (acc_s[...] / l_s[...]).astype(o_ref.dtype)


def _attention(qkv_p, bias_tabs):
    s = SEQ
    nt = bias_tabs.shape[1]
    heads = DIL_HEADS
    return pl.pallas_call(
        _attn_kernel,
        grid=(BATCH, heads),
        in_specs=[
            pl.BlockSpec((s, DIL_DH), lambda b, h: (b, h)),
            pl.BlockSpec((s, DIL_DH), lambda b, h: (b, heads + h)),
            pl.BlockSpec((s, DIL_DH), lambda b, h: (b, 2 * heads + h)),
            pl.BlockSpec((None, nt, DIL_BLOCK, 2 * DIL_BLOCK), lambda b, h: (h, 0, 0, 0)),
        ],
        out_specs=pl.BlockSpec((s, DIL_DH), lambda b, h: (b, h)),
        out_shape=jax.ShapeDtypeStruct((M_TOK, DIL_W), BF16),
        scratch_shapes=[
            pltpu.VMEM((s, DIL_DH), F32),
            pltpu.VMEM((s, DIL_DH), F32),
            pltpu.VMEM((s, DIL_DH), F32),
            pltpu.VMEM((s, DIL_DH), F32),
            pltpu.VMEM((s, LANE), F32),
            pltpu.VMEM((s, LANE), F32),
        ],
        compiler_params=_cparams(("parallel", "parallel")),
        name="dilated_attention",
    )(qkv_p, qkv_p, qkv_p, bias_tabs)


def _to_class_order(a):
    w = a.shape[1]
    return a.reshape(BATCH, CLASS_LEN, N_CLASS, w).transpose(0, 2, 1, 3).reshape(M_TOK, w)


def _to_token_order(a):
    w = a.shape[1]
    return a.reshape(BATCH, N_CLASS, CLASS_LEN, w).transpose(0, 2, 1, 3).reshape(M_TOK, w)


def _outproj_kernel(oa_ref, ob_ref, x_ref, w_ref, g_ref, x1_ref, h2_ref, wb_s):
    @pl.when(pl.program_id(0) == 0)
    def _():
        wb_s[...] = w_ref[...].astype(BF16)

    oa, ob = oa_ref[...], ob_ref[...]
    d = x_ref.shape[1]
    ssq = jnp.zeros((x_ref.shape[0], 1), F32)
    for n in range(d // OUT_CHUNK):
        cs = slice(n * OUT_CHUNK, (n + 1) * OUT_CHUNK)
        acc = jnp.dot(oa, wb_s[0:VA_W, cs], preferred_element_type=F32)
        acc = acc + jnp.dot(ob, wb_s[VA_W:, cs], preferred_element_type=F32)
        x1 = x_ref[:, cs] + acc
        x1_ref[:, cs] = x1
        ssq = ssq + jnp.sum(x1 * x1, axis=-1, keepdims=True)
    scale = lax.rsqrt(ssq * (1.0 / d) + RMS_EPS)
    h2_ref[...] = (x1_ref[...] * scale * g_ref[...]).astype(h2_ref.dtype)


def _outproj(out_a, out_b, x, w_out, layer, norm2_g):
    m, d = x.shape
    tm = TM_OUT
    row = lambda w: pl.BlockSpec((tm, w), lambda i: (i, 0))
    return pl.pallas_call(
        _outproj_kernel,
        grid=(m // tm,),
        in_specs=[row(VA_W), row(DIL_W), row(d),
                  pl.BlockSpec((None, d, d), lambda i: (layer, 0, 0), pipeline_mode=pl.Buffered(1)),
                  pl.BlockSpec((1, d), lambda i: (0, 0))],
        out_specs=[row(d), row(d)],
        out_shape=[jax.ShapeDtypeStruct((m, d), F32), jax.ShapeDtypeStruct((m, d), BF16)],
        scratch_shapes=[pltpu.VMEM((d, d), BF16)],
        compiler_params=_cparams(("arbitrary",)),
        name="outproj_norm2",
    )(out_a, out_b, x, w_out, norm2_g.reshape(1, d))


def _ffn_kernel(h_ref, x_hbm, wg_ref, wu_ref, wd_ref, o_ref, x_sem):
    i, j = pl.program_id(0), pl.program_id(1)
    tm = o_ref.shape[0]

    def hidden():
        h = h_ref[...]
        gate = jnp.dot(h, wg_ref[...].astype(BF16), preferred_element_type=F32)
        up = jnp.dot(h, wu_ref[...].astype(BF16), preferred_element_type=F32)
        return (gate * jax.nn.sigmoid(gate) * up).astype(BF16)

    def down(act):
        return jnp.dot(act, wd_ref[...].astype(BF16), preferred_element_type=F32)

    @pl.when(j == 0)
    def _():
        rows = pl.ds(pl.multiple_of(i * tm, tm), tm)
        residual = pltpu.make_async_copy(x_hbm.at[rows, :], o_ref, x_sem)
        residual.start()
        act = hidden()
        residual.wait()
        o_ref[...] += down(act)

    @pl.when(j > 0)
    def _():
        o_ref[...] += down(hidden())


def _ffn(h2, x1, w_gate, w_up, w_down, layer):
    m, d = x1.shape
    f = w_gate.shape[2]
    tm, tf = TM_FFN, TF_FFN
    return pl.pallas_call(
        _ffn_kernel,
        grid=(m // tm, f // tf),
        in_specs=[pl.BlockSpec((tm, d), lambda i, j: (i, 0)),
                  pl.BlockSpec(memory_space=pl.ANY),
                  pl.BlockSpec((None, d, tf), lambda i, j: (layer, 0, j)),
                  pl.BlockSpec((None, d, tf), lambda i, j: (layer, 0, j)),
                  pl.BlockSpec((None, tf, d), lambda i, j: (layer, j, 0))],
        out_specs=pl.BlockSpec((tm, d), lambda i, j: (i, 0)),
        out_shape=jax.ShapeDtypeStruct((m, d), F32),
        scratch_shapes=[pltpu.SemaphoreType.DMA(())],
        compiler_params=_cparams(("parallel", "arbitrary")),
        name="swiglu_ffn",
    )(h2, x1, w_gate, w_up, w_down)


def kernel(x, norm1_g, w_in, gla_gate_w2, gla_gate_b, gla_onorm_g, q_norm_g, k_norm_g, rel_bias,
           w_out, norm2_g, w_gate, w_up, w_down):
    bsz, s, d = x.shape
    assert (bsz, s, d) == (BATCH, SEQ, D_MODEL)
    bias_tabs = _bias_tables(rel_bias.astype(F32), jnp.asarray(_attention_tables()))
    w_in_t = jnp.swapaxes(w_in, 1, 2)

    xf = x.reshape(M_TOK, D_MODEL)
    for l in range(DEPTH):
        w_gate2 = jnp.pad(gla_gate_w2[l], ((0, GATE_PAD - GLA_GATE_RANK), (0, 0))).astype(BF16)
        qk_gain = jnp.concatenate([jnp.tile(q_norm_g[l], DIL_HEADS) * (DIL_DH ** -0.5 * LOG2E),
                                   jnp.tile(k_norm_g[l], DIL_HEADS)]).reshape(1, 2 * DIL_W).astype(F32)

        h, log_g = _norm_gate(xf, norm1_g[l], w_in_t, l, w_gate2,
                              gla_gate_b[l].reshape(1, QA_W).astype(F32))
        h_p = _to_class_order(h)
        pa = _proj_wt(h, w_in_t, l, 0, PA_W, name="proj_gla")
        h_p, pa = lax.optimization_barrier((h_p, pa))
        qkv_p = _proj_wt(h_p, w_in_t, l, PB_OFF, PB_W, gain=qk_gain, name="proj_qkv")
        out_b_p = _attention(qkv_p, bias_tabs)
        out_b_p, pa = lax.optimization_barrier((out_b_p, pa))
        out_b = _to_token_order(out_b_p)
        out_a = _gla(pa, log_g, gla_onorm_g[l])

        x1, h2 = _outproj(out_a, out_b, xf, w_out, l, norm2_g[l])
        xf = _ffn(h2, x1, w_gate, w_up, w_down, l)
    return xf.reshape(bsz, s, d)
```
